```python
import math
import jax, jax.numpy as jnp
from jax import lax
import numpy as np


D_MODEL = 2048
BATCH = 1
SEQ = 16384
DEPTH = 2
DEC_BATCH = 8
DEC_SEQ = 4096
PAST_LEN = 128

CHUNK = 128
D_A = D_MODEL // 2
H_A = 8
D_B = D_MODEL // 2
H_B = 8
CONV_W = 3
D_C = D_MODEL // 2
G_C = 8
GC_W = D_C // G_C
H_D = 8
DH_D = 64
DV_D = 2 * DH_D
D_D = H_D * DV_D
D_FF = 4 * D_MODEL
ROPE_THETA = 10000.0
LN_EPS = 1e-5
ALPHA = (2 * DEPTH) ** 0.25
BETA = (8 * DEPTH) ** -0.25
N_EVEN = (DEPTH + 1) // 2
N_ODD = DEPTH // 2
EVEN_IN = 2 * D_A + 3 * D_B
ODD_IN = D_C + 2 * (2 * H_D * DH_D) + D_D

kernel_name = 'hybrid_gmlp_conv_fnet_diffattn_encoder'


def layer_norm(x, g, b):
    xf = x.astype(jnp.float32)
    mu = jnp.mean(xf, axis=-1, keepdims=True)
    var = jnp.mean(jnp.square(xf - mu), axis=-1, keepdims=True)
    return ((xf - mu) * lax.rsqrt(var + LN_EPS) * g.astype(jnp.float32) + b.astype(jnp.float32)).astype(x.dtype)


def rms_norm(x, g):
    xf = x.astype(jnp.float32)
    ms = jnp.mean(jnp.square(xf), axis=-1, keepdims=True)
    return (xf * lax.rsqrt(ms + LN_EPS) * g.astype(jnp.float32)).astype(x.dtype)


def rope(x, pos):
    d = x.shape[-1]
    half = d // 2
    inv_freq = 1.0 / (ROPE_THETA ** (jnp.arange(half, dtype=jnp.float32) / half))
    ang = pos[:, None] * inv_freq[None, :]
    cos = jnp.cos(ang)[None, :, None, :]
    sin = jnp.sin(ang)[None, :, None, :]
    xf = x.astype(jnp.float32)
    x1, x2 = xf[..., :half], xf[..., half:]
    return jnp.concatenate([x1 * cos - x2 * sin, x2 * cos + x1 * sin], axis=-1).astype(x.dtype)


def spatial_gating(u, v, ln_g, ln_b, w_s, b_s):
    bsz, s, _ = v.shape
    v = layer_norm(v, ln_g, ln_b)
    v = v.reshape(bsz, s // CHUNK, CHUNK, H_A, D_A // H_A)
    mixed = jnp.einsum('hpq,bnqhc->bnphc', w_s, v) + b_s.T[None, None, :, :, None]
    return u * mixed.reshape(bsz, s, D_A)


def short_gated_conv(bg, cg, xp, conv_w):
    z = cg * xp
    s = z.shape[1]
    pad = CONV_W // 2
    zp = jnp.pad(z, ((0, 0), (pad, pad), (0, 0)))
    conv = sum(conv_w[k] * zp[:, k:k + s] for k in range(CONV_W))
    return bg * conv


def fourier_mix(xc):
    bsz, s, _ = xc.shape
    xg = xc.reshape(bsz, s, G_C, GC_W).astype(jnp.float32)
    f = jnp.fft.fft2(xg, axes=(1, 3), norm='ortho').real
    return f.reshape(bsz, s, D_C).astype(xc.dtype)


def diff_attention(q1, q2, k1, k2, v, lam):
    bsz, s, h, dh = q1.shape
    nblk = s // CHUNK
    scale = dh ** -0.5

    def to_blocks(q):
        return q.reshape(bsz, nblk, CHUNK, h, dh).transpose(1, 0, 2, 3, 4)

    def block(qs):
        qb1, qb2 = qs
        s1 = jnp.einsum('bqhd,bkhd->bhqk', qb1, k1).astype(jnp.float32) * scale
        s2 = jnp.einsum('bqhd,bkhd->bhqk', qb2, k2).astype(jnp.float32) * scale
        p = jax.nn.softmax(s1, axis=-1) - lam * jax.nn.softmax(s2, axis=-1)
        return jnp.einsum('bhqk,bkhe->bqhe', p.astype(v.dtype), v)

    out = lax.map(block, (to_blocks(q1), to_blocks(q2)))
    return out.transpose(1, 0, 2, 3, 4).reshape(bsz, s, h, v.shape[-1])


def even_mixer(x, w_in, sgu_ln_g, sgu_ln_b, sgu_w, sgu_b, conv_w, w_out):
    hcat = jnp.einsum('bsd,de->bse', x, w_in)
    za = jax.nn.gelu(hcat[..., :2 * D_A])
    u, v = za[..., :D_A], za[..., D_A:]
    o = 2 * D_A
    bg = hcat[..., o:o + D_B]
    cg = hcat[..., o + D_B:o + 2 * D_B]
    xp = hcat[..., o + 2 * D_B:]
    ya = spatial_gating(u, v, sgu_ln_g, sgu_ln_b, sgu_w, sgu_b)
    yb = short_gated_conv(bg, cg, xp, conv_w)
    return jnp.einsum('bse,ed->bsd', jnp.concatenate([ya, yb], axis=-1), w_out)


def odd_mixer(x, w_in, lq1, lk1, lq2, lk2, subln_g, w_out, lam_init):
    bsz, s, _ = x.shape
    hcat = jnp.einsum('bsd,de->bse', x, w_in)
    nqk = 2 * H_D * DH_D
    xc = hcat[..., :D_C]
    q = hcat[..., D_C:D_C + nqk]
    k = hcat[..., D_C + nqk:D_C + 2 * nqk]
    v = hcat[..., D_C + 2 * nqk:].reshape(bsz, s, H_D, DV_D)
    yc = fourier_mix(xc)
    pos = jnp.arange(s, dtype=jnp.float32)
    q = rope(q.reshape(bsz, s, 2 * H_D, DH_D), pos).reshape(bsz, s, H_D, 2, DH_D)
    k = rope(k.reshape(bsz, s, 2 * H_D, DH_D), pos).reshape(bsz, s, H_D, 2, DH_D)
    lam = (jnp.exp(jnp.sum(lq1.astype(jnp.float32) * lk1.astype(jnp.float32)))
           - jnp.exp(jnp.sum(lq2.astype(jnp.float32) * lk2.astype(jnp.float32))) + lam_init)
    od = diff_attention(q[:, :, :, 0], q[:, :, :, 1], k[:, :, :, 0], k[:, :, :, 1], v, lam)
    od = rms_norm(od, subln_g) * (1.0 - lam_init)
    yd = od.reshape(bsz, s, D_D)
    return jnp.einsum('bse,ed->bsd', jnp.concatenate([yc, yd], axis=-1), w_out)


def trunk(x, w_in_ab, sgu_ln_g, sgu_ln_b, sgu_w, sgu_b, conv_w, w_out_ab,
          w_in_cd, lambda_q1, lambda_k1, lambda_q2, lambda_k2, subln_g, w_out_cd,
          ln1_g, ln1_b, ln2_g, ln2_b, w_ff1, w_ff2):
    for l in range(DEPTH):
        if l % 2 == 0:
            i = l // 2
            m = even_mixer(x, w_in_ab[i], sgu_ln_g[i], sgu_ln_b[i], sgu_w[i], sgu_b[i],
                           conv_w[i], w_out_ab[i])
        else:
            i = l // 2
            lam_init = 0.8 - 0.6 * math.exp(-0.3 * l)
            m = odd_mixer(x, w_in_cd[i], lambda_q1[i], lambda_k1[i], lambda_q2[i], lambda_k2[i],
                          subln_g[i], w_out_cd[i], lam_init)
        x = layer_norm(ALPHA * x + m, ln1_g[l], ln1_b[l])
        hdn = jnp.square(jax.nn.relu(jnp.einsum('bsd,df->bsf', x, w_ff1[l])))
        f = jnp.einsum('bsf,fd->bsd', hdn, w_ff2[l])
        x = layer_norm(ALPHA * x + f, ln2_g[l], ln2_b[l])
    return x


def setup_inputs(seed: int = 0) -> dict:
    key = jax.random.key(seed)
    ks = jax.random.split(key, 24)
    f32 = jnp.float32
    nrm = lambda k, shape, sc: jax.random.normal(k, shape, f32) * sc
    return {
        'x_prompt': nrm(ks[0], (BATCH, SEQ, D_MODEL), 1.0),
        'x_sample': nrm(ks[1], (DEC_BATCH, DEC_SEQ, D_MODEL), 1.0),
        'w_in_ab': nrm(ks[2], (N_EVEN, D_MODEL, EVEN_IN), D_MODEL ** -0.5),
        'sgu_ln_g': 1.0 + nrm(ks[3], (N_EVEN, D_A), 0.02),
        'sgu_ln_b': nrm(ks[4], (N_EVEN, D_A), 0.02),
        'sgu_w': nrm(ks[5], (N_EVEN, H_A, CHUNK, CHUNK), CHUNK ** -0.5),
        'sgu_b': 1.0 + nrm(ks[6], (N_EVEN, H_A, CHUNK), 0.01),
        'conv_w': nrm(ks[7], (N_EVEN, CONV_W, D_B), CONV_W ** -0.5),
        'w_out_ab': nrm(ks[8], (N_EVEN, D_A + D_B, D_MODEL), BETA * (D_A + D_B) ** -0.5),
        'w_in_cd': nrm(ks[9], (N_ODD, D_MODEL, ODD_IN), D_MODEL ** -0.5),
        'lambda_q1': nrm(ks[10], (N_ODD, DH_D), 0.1),
        'lambda_k1': nrm(ks[11], (N_ODD, DH_D), 0.1),
        'lambda_q2': nrm(ks[12], (N_ODD, DH_D), 0.1),
        'lambda_k2': nrm(ks[13], (N_ODD, DH_D), 0.1),
        'subln_g': 1.0 + nrm(ks[14], (N_ODD, DV_D), 0.02),
        'w_out_cd': nrm(ks[15], (N_ODD, D_C + D_D, D_MODEL), BETA * (D_C + D_D) ** -0.5),
        'ln1_g': 1.0 + nrm(ks[16], (DEPTH, D_MODEL), 0.02),
        'ln1_b': nrm(ks[17], (DEPTH, D_MODEL), 0.02),
        'ln2_g': 1.0 + nrm(ks[18], (DEPTH, D_MODEL), 0.02),
        'ln2_b': nrm(ks[19], (DEPTH, D_MODEL), 0.02),
        'w_ff1': nrm(ks[20], (DEPTH, D_MODEL, D_FF), D_MODEL ** -0.5),
        'w_ff2': nrm(ks[21], (DEPTH, D_FF, D_MODEL), BETA * D_FF ** -0.5),
    }


def reference(x_prompt, x_sample, w_in_ab, sgu_ln_g, sgu_ln_b, sgu_w, sgu_b, conv_w, w_out_ab,
              w_in_cd, lambda_q1, lambda_k1, lambda_q2, lambda_k2, subln_g, w_out_cd,
              ln1_g, ln1_b, ln2_g, ln2_b, w_ff1, w_ff2):
    y_prompt = trunk(x_prompt, w_in_ab, sgu_ln_g, sgu_ln_b, sgu_w, sgu_b, conv_w, w_out_ab,
                     w_in_cd, lambda_q1, lambda_k1, lambda_q2, lambda_k2, subln_g, w_out_cd,
                     ln1_g, ln1_b, ln2_g, ln2_b, w_ff1, w_ff2)
    y_sample = trunk(x_sample, w_in_ab, sgu_ln_g, sgu_ln_b, sgu_w, sgu_b, conv_w, w_out_ab,
                     w_in_cd, lambda_q1, lambda_k1, lambda_q2, lambda_k2, subln_g, w_out_cd,
                     ln1_g, ln1_b, ln2_g, ln2_b, w_ff1, w_ff2)
    return (y_prompt, y_sample)
```

```python
import functools
import math

import numpy as np
import jax
import jax.numpy as jnp
from jax import lax
from jax.experimental import pallas as pl
from jax.experimental.pallas import tpu as pltpu

D_MODEL = 2048
DEPTH = 2
CHUNK = 128
D_A = D_MODEL // 2
H_A = 8
D_B = D_MODEL // 2
D_C = D_MODEL // 2
GC_W = 128
H_D = 8
DH_D = 64
DV_D = 2 * DH_D
D_D = H_D * DV_D
D_FF = 4 * D_MODEL
ROPE_THETA = 10000.0
LN_EPS = 1e-5
ALPHA = (2 * DEPTH) ** 0.25

LANES = 128
VMEM_LIMIT_BYTES = 56 * 1024 * 1024

BF16 = jnp.bfloat16
F32 = jnp.float32


def _params(semantics):
    return pltpu.CompilerParams(dimension_semantics=semantics,
                                vmem_limit_bytes=VMEM_LIMIT_BYTES)


def _dot(a, b):
    return jnp.dot(a, b, preferred_element_type=F32)


def _split(x):
    hi = x.astype(BF16)
    lo = (x - hi.astype(F32)).astype(BF16)
    return hi, lo


def _dot3(a_hi, a_lo, b_hi, b_lo):
    return _dot(a_hi, b_hi) + (_dot(a_lo, b_hi) + _dot(a_hi, b_lo))


def _layer_norm(x, g, b):
    mu = jnp.mean(x, axis=-1, keepdims=True)
    xc = x - mu
    var = jnp.mean(xc * xc, axis=-1, keepdims=True)
    return xc * lax.rsqrt(var + LN_EPS) * g + b


def _even_in_kernel(x_ref, w_ref, o_ref, xb_ref):
    j = pl.program_id(1)

    @pl.when(j == 0)
    def _():
        xb_ref[...] = x_ref[...].astype(BF16)

    acc = _dot(xb_ref[...], w_ref[...])

    @pl.when(j < 2)
    def _():
        o_ref[...] = jax.nn.gelu(acc)

    @pl.when(j >= 2)
    def _():
        o_ref[...] = acc


def _even_in_proj(x, w, tm=512, tn=1024):
    m, d = x.shape
    n = w.shape[1]
    return pl.pallas_call(
        _even_in_kernel,
        out_shape=jax.ShapeDtypeStruct((m, n), F32),
        grid=(m // tm, n // tn),
        in_specs=[pl.BlockSpec((tm, d), lambda i, j: (i, 0)),
                  pl.BlockSpec((d, tn), lambda i, j: (0, j))],
        out_specs=pl.BlockSpec((tm, tn), lambda i, j: (i, j)),
        scratch_shapes=[pltpu.VMEM((tm, d), BF16)],
        compiler_params=_params(("parallel", "arbitrary")),
        name="even_in_proj",
    )(x, w)


def _even_mix_kernel(u_ref, v_ref, bg_ref, cg_ref, xp_ref, cgp_ref, xpp_ref, cgn_ref, xpn_ref,
                     lng_ref, lnb_ref, ws_ref, bs_ref, cw_ref, ya_ref, yb_ref, *, tiles_per_seq):
    tm = u_ref.shape[0]
    i = pl.program_id(0)
    t = i % tiles_per_seq
    not_first = (t != 0).astype(F32)
    not_last = (t != tiles_per_seq - 1).astype(F32)

    z = cg_ref[...] * xp_ref[...]
    z_before = cgp_ref[7:8, :] * xpp_ref[7:8, :] * not_first
    z_after = cgn_ref[0:1, :] * xpn_ref[0:1, :] * not_last
    row = lax.broadcasted_iota(jnp.int32, (tm, 1), 0)
    z_dn = jnp.where(row == 0, z_before, pltpu.roll(z, 1, axis=0))
    z_up = jnp.where(row == tm - 1, z_after, pltpu.roll(z, tm - 1, axis=0))
    conv = cw_ref[0:1, :] * z_dn + cw_ref[1:2, :] * z + cw_ref[2:3, :] * z_up
    yb_ref[...] = (bg_ref[...] * conv).astype(BF16)

    vn = _layer_norm(v_ref[...], lng_ref[...], lnb_ref[...]).astype(BF16)
    for c in range(tm // CHUNK):
        rows = slice(c * CHUNK, (c + 1) * CHUNK)
        for h in range(H_A):
            cols = slice(h * LANES, (h + 1) * LANES)
            mixed = _dot(ws_ref[h], vn[rows, cols]) + bs_ref[h]
            ya_ref[rows, cols] = (u_ref[rows, cols] * mixed).astype(BF16)


def _even_mix(hcat, seq, ln_g, ln_b, ws, bs, cw, tm=512):
    m = hcat.shape[0]
    w = D_A
    nb8 = m // 8
    r8 = tm // 8
    col = lambda c: pl.BlockSpec((tm, w), lambda i, c=c: (i, c))
    prev = lambda c: pl.BlockSpec((8, w), lambda i, c=c: (jnp.maximum(i * r8 - 1, 0), c))
    nxt = lambda c: pl.BlockSpec((8, w), lambda i, c=c: (jnp.minimum((i + 1) * r8, nb8 - 1), c))
    full = lambda a: pl.BlockSpec(a.shape, lambda i: (0,) * a.ndim)
    return pl.pallas_call(
        functools.partial(_even_mix_kernel, tiles_per_seq=seq // tm),
        out_shape=(jax.ShapeDtypeStruct((m, w), BF16), jax.ShapeDtypeStruct((m, w), BF16)),
        grid=(m // tm,),
        in_specs=[col(0), col(1), col(2), col(3), col(4), prev(3), prev(4), nxt(3), nxt(4),
                  full(ln_g), full(ln_b), full(ws), full(bs), full(cw)],
        out_specs=(pl.BlockSpec((tm, w), lambda i: (i, 0)), pl.BlockSpec((tm, w), lambda i: (i, 0))),
        compiler_params=_params(("parallel",)),
        name="even_mix",
    )(hcat, hcat, hcat, hcat, hcat, hcat, hcat, hcat, hcat, ln_g, ln_b, ws, bs, cw)


def _out_proj_kernel(a1_ref, a2_ref, x_ref, w1_ref, w2_ref, g_ref, b_ref, o_ref):
    mix = _dot(a1_ref[...], w1_ref[...]) + _dot(a2_ref[...], w2_ref[...])
    o_ref[...] = _layer_norm(ALPHA * x_ref[...] + mix, g_ref[...], b_ref[...])


def _out_proj_ln(a1, a2, x, w1, w2, g, b, tm=512):
    m, d = x.shape
    k1, k2 = a1.shape[1], a2.shape[1]
    full = lambda a: pl.BlockSpec(a.shape, lambda i: (0,) * a.ndim)
    return pl.pallas_call(
        _out_proj_kernel,
        out_shape=jax.ShapeDtypeStruct((m, d), F32),
        grid=(m // tm,),
        in_specs=[pl.BlockSpec((tm, k1), lambda i: (i, 0)), pl.BlockSpec((tm, k2), lambda i: (i, 0)),
                  pl.BlockSpec((tm, d), lambda i: (i, 0)), full(w1), full(w2), full(g), full(b)],
        out_specs=pl.BlockSpec((tm, d), lambda i: (i, 0)),
        compiler_params=_params(("parallel",)),
        name="out_proj_ln",
    )(a1, a2, x, w1, w2, g, b)


def _ffn_kernel(x_ref, w1_ref, w2_ref, g_ref, b_ref, o_ref, xb_ref):
    j = pl.program_id(1)

    @pl.when(j == 0)
    def _():
        x = x_ref[...]
        xb_ref[...] = x.astype(BF16)
        o_ref[...] = ALPHA * x

    h = jnp.maximum(_dot(xb_ref[...], w1_ref[...]), 0.0)
    o_ref[...] += _dot((h * h).astype(BF16), w2_ref[...])

    @pl.when(j == pl.num_programs(1) - 1)
    def _():
        o_ref[...] = _layer_norm(o_ref[...], g_ref[...], b_ref[...])


def _ffn_ln(x, w1, w2, g, b, tm=512, tf=512):
    m, d = x.shape
    f = w1.shape[1]
    full = lambda a: pl.BlockSpec(a.shape, lambda i, j: (0,) * a.ndim)
    return pl.pallas_call(
        _ffn_kernel,
        out_shape=jax.ShapeDtypeStruct((m, d), F32),
        grid=(m // tm, f // tf),
        in_specs=[pl.BlockSpec((tm, d), lambda i, j: (i, 0)),
                  pl.BlockSpec((d, tf), lambda i, j: (0, j)),
                  pl.BlockSpec((tf, d), lambda i, j: (j, 0)),
                  full(g), full(b)],
        out_specs=pl.BlockSpec((tm, d), lambda i, j: (i, 0)),
        scratch_shapes=[pltpu.VMEM((tm, d), BF16)],
        compiler_params=_params(("parallel", "arbitrary")),
        name="ffn_ln",
    )(x, w1, w2, g, b)


def _rope(x, cos, sin_signed, is_first_half):
    partner = jnp.where(is_first_half, pltpu.roll(x, LANES - DH_D // 2, axis=1),
                        pltpu.roll(x, DH_D // 2, axis=1))
    return x * cos + partner * sin_signed


def _odd_in_kernel(x_ref, w_ref, cos_ref, sin_ref, xc_ref, qt_ref, k_ref, vt_ref, xb_ref, *, tq):
    tm = x_ref.shape[0]
    j = pl.program_id(1)

    @pl.when(j == 0)
    def _():
        xb_ref[...] = x_ref[...].astype(BF16)

    acc = _dot(xb_ref[...], w_ref[...])
    lane = lax.broadcasted_iota(jnp.int32, (tm, LANES), 1)
    is_first_half = (lane % DH_D) < (DH_D // 2)

    @pl.when(j == 0)
    def _():
        xc_ref[...] = acc

    @pl.when(j == 1)
    def _():
        for h in range(H_D):
            cols = slice(h * LANES, (h + 1) * LANES)
            q = _rope(acc[:, cols], cos_ref[...], sin_ref[...], is_first_half) * (DH_D ** -0.5)
            for c in range(tm // tq):
                qt_ref[c, cols, :] = q[c * tq:(c + 1) * tq, :].T.astype(BF16)

    @pl.when(j == 2)
    def _():
        for h in range(H_D):
            cols = slice(h * LANES, (h + 1) * LANES)
            k_ref[:, cols] = _rope(acc[:, cols], cos_ref[...], sin_ref[...], is_first_half).astype(BF16)

    @pl.when(j == 3)
    def _():
        for h in range(H_D):
            cols = slice(h * LANES, (h + 1) * LANES)
            vt_ref[0, cols, :] = acc[:, cols].T.astype(BF16)


def _odd_in_proj(x, w, cos, sin_signed, seq, tq, tm=512):
    m, d = x.shape
    tn = D_C
    spt = seq // tm
    return pl.pallas_call(
        functools.partial(_odd_in_kernel, tq=tq),
        out_shape=(jax.ShapeDtypeStruct((m, D_C), F32),
                   jax.ShapeDtypeStruct((m // tq, D_D, tq), BF16),
                   jax.ShapeDtypeStruct((m, D_D), BF16),
                   jax.ShapeDtypeStruct((m // tm, D_D, tm), BF16)),
        grid=(m // tm, 4),
        in_specs=[pl.BlockSpec((tm, d), lambda i, j: (i, 0)),
                  pl.BlockSpec((d, tn), lambda i, j: (0, j)),
                  pl.BlockSpec((tm, LANES), lambda i, j: (i % spt, 0)),
                  pl.BlockSpec((tm, LANES), lambda i, j: (i % spt, 0))],
        out_specs=(pl.BlockSpec((tm, D_C), lambda i, j: (i, 0)),
                   pl.BlockSpec((tm // tq, D_D, tq), lambda i, j: (i, 0, 0)),
                   pl.BlockSpec((tm, D_D), lambda i, j: (i, 0)),
                   pl.BlockSpec((1, D_D, tm), lambda i, j: (i, 0, 0))),
        scratch_shapes=[pltpu.VMEM((tm, d), BF16)],
        compiler_params=_params(("parallel", "arbitrary")),
        name="odd_in_proj",
    )(x, w, cos, sin_signed)


def _fft_stage1_kernel(x_ref, f1h_ref, f1l_ref, wch_ref, wcl_ref, o_ref):
    n1 = x_ref.shape[1]
    groups = x_ref.shape[2] // LANES
    xh, xl = _split(x_ref[0])
    y = _dot3(f1h_ref[...], f1l_ref[...], xh, xl)
    yr = jnp.concatenate([y[:n1, g * LANES:(g + 1) * LANES] for g in range(groups)], axis=0)
    yi = jnp.concatenate([y[n1:, g * LANES:(g + 1) * LANES] for g in range(groups)], axis=0)
    ych, ycl = _split(jnp.concatenate([yr, yi], axis=1))
    u = _dot3(ych, ycl, wch_ref[...], wcl_ref[...])
    for g in range(groups):
        o_ref[0, 0, :, g * LANES:(g + 1) * LANES] = u[g * n1:(g + 1) * n1, :LANES]
        o_ref[0, 1, :, g * LANES:(g + 1) * LANES] = u[g * n1:(g + 1) * n1, LANES:]


def _fft_stage2_kernel(u_ref, f2c_ref, f2s_ref, tc_ref, ts_ref, o_ref, *, scale):
    kb = u_ref.shape[2]
    base = pl.program_id(1) * kb
    f2c = f2c_ref[...]
    f2s = f2s_ref[...]
    for kk in range(kb):
        tc = tc_ref[pl.ds(base + kk, 1), :]
        ts = ts_ref[pl.ds(base + kk, 1), :]
        gc = (f2c * tc - f2s * ts) * scale
        gs = (f2s * tc + f2c * ts) * scale
        gh, gl = _split(jnp.concatenate([gc, gs], axis=1))
        dh, dl = _split(jnp.concatenate([u_ref[0, 0, kk], u_ref[0, 1, kk]], axis=0))
        o_ref[0, :, kk * D_C:(kk + 1) * D_C] = _dot3(gh, gl, dh, dl).astype(BF16)


def _dft_tables(seq):
    n2 = CHUNK
    n1 = seq // n2
    two_pi = 2.0 * np.pi
    a1 = two_pi * np.outer(np.arange(n1), np.arange(n1)) / n1
    f1 = np.concatenate([np.cos(a1), -np.sin(a1)], axis=0)
    ac = two_pi * np.outer(np.arange(GC_W), np.arange(GC_W)) / GC_W
    wc = np.block([[np.cos(ac), -np.sin(ac)], [np.sin(ac), np.cos(ac)]])
    a2 = two_pi * np.outer(np.arange(n2), np.arange(n2)) / n2
    at = two_pi * np.outer(np.arange(n1), np.arange(n2)) / seq

    def split(a):
        a = jnp.asarray(a, F32)
        hi = a.astype(BF16)
        return hi, (a - hi.astype(F32)).astype(BF16)

    f32 = lambda a: jnp.asarray(a, F32)
    return split(f1) + split(wc) + (f32(np.cos(a2)), f32(np.sin(a2)), f32(np.cos(at)), f32(np.sin(at)))


def _fourier_mix(xc, batch, seq, jb=8, kb=4):
    n2 = CHUNK
    n1 = seq // n2
    f1h, f1l, wch, wcl, f2c, f2s, tc, ts = _dft_tables(seq)
    full2 = lambda a: pl.BlockSpec(a.shape, lambda b, i: (0,) * a.ndim)
    wblk = jb * D_C
    u = pl.pallas_call(
        _fft_stage1_kernel,
        out_shape=jax.ShapeDtypeStruct((batch, 2, n1, n2 * D_C), F32),
        grid=(batch, n2 // jb),
        in_specs=[pl.BlockSpec((1, n1, wblk), lambda b, i: (b, 0, i)),
                  full2(f1h), full2(f1l), full2(wch), full2(wcl)],
        out_specs=pl.BlockSpec((1, 2, n1, wblk), lambda b, i: (b, 0, 0, i)),
        compiler_params=_params(("parallel", "parallel")),
        name="fft_stage1",
    )(xc.reshape(batch, n1, n2 * D_C), f1h, f1l, wch, wcl)
    kb = min(kb, n1)
    yc = pl.pallas_call(
        functools.partial(_fft_stage2_kernel, scale=float(1.0 / math.sqrt(seq * GC_W))),
        out_shape=jax.ShapeDtypeStruct((batch, n2, n1 * D_C), BF16),
        grid=(batch, n1 // kb),
        in_specs=[pl.BlockSpec((1, 2, kb, n2, D_C), lambda b, i: (b, 0, i, 0, 0)),
                  full2(f2c), full2(f2s), full2(tc), full2(ts)],
        out_specs=pl.BlockSpec((1, n2, kb * D_C), lambda b, i: (b, 0, i)),
        compiler_params=_params(("parallel", "parallel")),
        name="fft_stage2",
    )(u.reshape(batch, 2, n1, n2, D_C), f2c, f2s, tc, ts)
    return yc.reshape(batch * seq, D_C)


def _attn_kernel(qt_ref, k_ref, vt_ref, lq1_ref, lk1_ref, lq2_ref, lk2_ref, g_ref, o_ref,
                 m_ref, l_ref, acc_ref, *, lam_init):
    tq = qt_ref.shape[2]
    tk = vt_ref.shape[2]
    nk = vt_ref.shape[0]

    qt = qt_ref[0]
    row = lax.broadcasted_iota(jnp.int32, (LANES, tq), 0)
    zero = jnp.zeros_like(qt)
    q2 = jnp.concatenate([jnp.where(row < DH_D, qt, zero), jnp.where(row >= DH_D, qt, zero)], axis=1)

    m_ref[...] = jnp.full(m_ref.shape, -1e30, F32)
    l_ref[...] = jnp.zeros(l_ref.shape, F32)
    acc_ref[...] = jnp.zeros(acc_ref.shape, F32)

    def body(c, carry):
        kc = k_ref[pl.ds(pl.multiple_of(c * tk, tk), tk), :]
        s = _dot(kc, q2)
        m_old = m_ref[...]
        m_new = jnp.maximum(m_old, jnp.max(s, axis=0, keepdims=True))
        alpha = jnp.exp(m_old - m_new)
        p = jnp.exp(s - m_new)
        l_ref[...] = alpha * l_ref[...] + jnp.sum(p, axis=0, keepdims=True)
        acc_ref[...] = alpha * acc_ref[...] + _dot(vt_ref[c], p.astype(BF16))
        m_ref[...] = m_new
        return carry

    lax.fori_loop(0, nk, body, 0)

    lam = (jnp.exp(jnp.sum(lq1_ref[...] * lk1_ref[...], axis=1, keepdims=True))
           - jnp.exp(jnp.sum(lq2_ref[...] * lk2_ref[...], axis=1, keepdims=True)) + lam_init)
    on = acc_ref[...] / l_ref[...]
    d = on[:, :tq] - lam * on[:, tq:]
    ms = jnp.mean(d * d, axis=0, keepdims=True)
    y = d * lax.rsqrt(ms + LN_EPS) * g_ref[...] * (1.0 - lam_init)
    o_ref[...] = y.T.astype(BF16)


def _diff_attention(qt, k, vt, lq1, lk1, lq2, lk2, g, batch, seq, lam_init):
    tq = qt.shape[2]
    tk = vt.shape[2]
    m = batch * seq
    nq = seq // tq
    gb = jnp.broadcast_to(g.astype(F32).reshape(DV_D, 1), (DV_D, tq))
    vec = lambda a: a.astype(F32).reshape(1, DH_D)
    full = lambda a: pl.BlockSpec(a.shape, lambda b, h, i: (0,) * a.ndim)
    args = (qt, k, vt, vec(lq1), vec(lk1), vec(lq2), vec(lk2), gb)
    return pl.pallas_call(
        functools.partial(_attn_kernel, lam_init=lam_init),
        out_shape=jax.ShapeDtypeStruct((m, D_D), BF16),
        grid=(batch, H_D, nq),
        in_specs=[pl.BlockSpec((1, DV_D, tq), lambda b, h, i: (b * nq + i, h, 0)),
                  pl.BlockSpec((seq, DV_D), lambda b, h, i: (b, h)),
                  pl.BlockSpec((seq // tk, DV_D, tk), lambda b, h, i: (b, h, 0)),
                  full(args[3]), full(args[4]), full(args[5]), full(args[6]), full(gb)],
        out_specs=pl.BlockSpec((tq, DV_D), lambda b, h, i: (b * nq + i, h)),
        scratch_shapes=[pltpu.VMEM((1, 2 * tq), F32), pltpu.VMEM((1, 2 * tq), F32),
                        pltpu.VMEM((DV_D, 2 * tq), F32)],
        compiler_params=_params(("parallel", "parallel", "arbitrary")),
        name="diff_attention",
    )(*args)


def _rope_tables(seq):
    half = DH_D // 2
    inv_freq = 1.0 / (ROPE_THETA ** (jnp.arange(half, dtype=F32) / half))
    ang = jnp.arange(seq, dtype=F32)[:, None] * inv_freq[None, :]
    cos = jnp.tile(jnp.cos(ang), (1, LANES // half))
    sign = jnp.tile(jnp.concatenate([-jnp.ones((half,), F32), jnp.ones((half,), F32)]), LANES // DH_D)
    sin_signed = jnp.tile(jnp.sin(ang), (1, LANES // half)) * sign[None, :]
    return cos, sin_signed


def _trunk(x3, p):
    batch, seq, d = x3.shape
    x = x3.reshape(batch * seq, d)
    row = lambda a: a.astype(F32).reshape(1, -1)
    for l in range(DEPTH):
        i = l // 2
        if l % 2 == 0:
            hcat = _even_in_proj(x, p["w_in_ab"][i])
            bs = jnp.broadcast_to(p["sgu_b"][i].astype(F32)[:, :, None], (H_A, CHUNK, LANES))
            ya, yb = _even_mix(hcat, seq, row(p["sgu_ln_g"][i]), row(p["sgu_ln_b"][i]),
                               p["sgu_w"][i], bs, p["conv_w"][i].astype(F32))
            w_out = p["w_out_ab"][i]
            x = _out_proj_ln(ya, yb, x, w_out[:D_A], w_out[D_A:], row(p["ln1_g"][l]), row(p["ln1_b"][l]))
        else:
            lam_init = 0.8 - 0.6 * math.exp(-0.3 * l)
            cos, sin_signed = _rope_tables(seq)
            xc, qt, k, vt = _odd_in_proj(x, p["w_in_cd"][i], cos, sin_signed, seq, tq=256)
            yc = _fourier_mix(xc, batch, seq)
            yd = _diff_attention(qt, k, vt, p["lambda_q1"][i], p["lambda_k1"][i], p["lambda_q2"][i],
                                 p["lambda_k2"][i], p["subln_g"][i], batch, seq, lam_init)
            w_out = p["w_out_cd"][i]
            x = _out_proj_ln(yc, yd, x, w_out[:D_C], w_out[D_C:], row(p["ln1_g"][l]), row(p["ln1_b"][l]))
        x = _ffn_ln(x, p["w_ff1"][l], p["w_ff2"][l], row(p["ln2_g"][l]), row(p["ln2_b"][l]))
    return x.reshape(batch, seq, d)


def kernel(x_prompt, x_sample, w_in_ab, sgu_ln_g, sgu_ln_b, sgu_w, sgu_b, conv_w, w_out_ab, w_in_cd,
           lambda_q1, lambda_k1, lambda_q2, lambda_k2, subln_g, w_out_cd, ln1_g, ln1_b, ln2_g, ln2_b,
           w_ff1, w_ff2):
    p = dict(w_in_ab=w_in_ab.astype(BF16), sgu_ln_g=sgu_ln_g, sgu_ln_b=sgu_ln_b, sgu_w=sgu_w.astype(BF16),
             sgu_b=sgu_b, conv_w=conv_w, w_out_ab=w_out_ab.astype(BF16), w_in_cd=w_in_cd.astype(BF16),
             lambda_q1=lambda_q1, lambda_k1=lambda_k1, lambda_q2=lambda_q2, lambda_k2=lambda_k2,
             subln_g=subln_g, w_out_cd=w_out_cd.astype(BF16), ln1_g=ln1_g, ln1_b=ln1_b, ln2_g=ln2_g,
             ln2_b=ln2_b, w_ff1=w_ff1.astype(BF16), w_ff2=w_ff2.astype(BF16))
    return (_trunk(x_prompt, p), _trunk(x_sample, p))
```

```python
import functools
import math

import numpy as np
import jax
import jax.numpy as jnp
from jax import lax
from jax.experimental import pallas as pl
from jax.experimental.pallas import tpu as pltpu

D_MODEL = 2048
DEPTH = 2
CHUNK = 128
D_A = D_MODEL // 2
H_A = 8
D_B = D_MODEL // 2
D_C = D_MODEL // 2
GC_W = 128
H_D = 8
DH_D = 64
DV_D = 2 * DH_D
D_D = H_D * DV_D
D_FF = 4 * D_MODEL
ROPE_THETA = 10000.0
LN_EPS = 1e-5
ALPHA = (2 * DEPTH) ** 0.25
Q_SCALE = DH_D ** -0.5 * math.log2(math.e)
VT_ROWS = DV_D + 16

LANES = 128
VMEM_LIMIT_BYTES = 56 * 1024 * 1024

BF16 = jnp.bfloat16
F32 = jnp.float32


def _params(semantics):
    return pltpu.CompilerParams(dimension_semantics=semantics,
                                vmem_limit_bytes=VMEM_LIMIT_BYTES)


def _dot(a, b):
    return jnp.dot(a, b, preferred_element_type=F32)


def _split(x):
    hi = x.astype(BF16)
    lo = (x - hi.astype(F32)).astype(BF16)
    return hi, lo


def _dot3(a_hi, a_lo, b_hi, b_lo):
    return _dot(a_hi, b_hi) + (_dot(a_lo, b_hi) + _dot(a_hi, b_lo))


def _layer_norm(x, g, b):
    mu = jnp.mean(x, axis=-1, keepdims=True)
    xc = x - mu
    var = jnp.mean(xc * xc, axis=-1, keepdims=True)
    return xc * lax.rsqrt(var + LN_EPS) * g + b


def _even_in_kernel(x_ref, w_ref, o_ref, xb_ref):
    j = pl.program_id(1)

    @pl.when(j == 0)
    def _():
        xb_ref[...] = x_ref[...].astype(BF16)

    acc = _dot(xb_ref[...], w_ref[...])

    @pl.when(j < 2)
    def _():
        o_ref[...] = jax.nn.gelu(acc)

    @pl.when(j >= 2)
    def _():
        o_ref[...] = acc


def _even_in_proj(x, w, tm=512, tn=1024):
    m, d = x.shape
    n = w.shape[1]
    return pl.pallas_call(
        _even_in_kernel,
        out_shape=jax.ShapeDtypeStruct((m, n), F32),
        grid=(m // tm, n // tn),
        in_specs=[pl.BlockSpec((tm, d), lambda i, j: (i, 0)),
                  pl.BlockSpec((d, tn), lambda i, j: (0, j))],
        out_specs=pl.BlockSpec((tm, tn), lambda i, j: (i, j)),
        scratch_shapes=[pltpu.VMEM((tm, d), BF16)],
        compiler_params=_params(("parallel", "arbitrary")),
        name="even_in_proj",
    )(x, w)


def _even_mix_kernel(u_ref, v_ref, bg_ref, cg_ref, xp_ref, cgp_ref, xpp_ref, cgn_ref, xpn_ref,
                     lng_ref, lnb_ref, ws_ref, bs_ref, cw_ref, ya_ref, yb_ref, *, tiles_per_seq):
    tm = u_ref.shape[0]
    i = pl.program_id(0)
    t = i % tiles_per_seq
    not_first = (t != 0).astype(F32)
    not_last = (t != tiles_per_seq - 1).astype(F32)

    z = cg_ref[...] * xp_ref[...]
    z_before = cgp_ref[7:8, :] * xpp_ref[7:8, :] * not_first
    z_after = cgn_ref[0:1, :] * xpn_ref[0:1, :] * not_last
    row = lax.broadcasted_iota(jnp.int32, (tm, 1), 0)
    z_dn = jnp.where(row == 0, z_before, pltpu.roll(z, 1, axis=0))
    z_up = jnp.where(row == tm - 1, z_after, pltpu.roll(z, tm - 1, axis=0))
    conv = cw_ref[0:1, :] * z_dn + cw_ref[1:2, :] * z + cw_ref[2:3, :] * z_up
    yb_ref[...] = (bg_ref[...] * conv).astype(BF16)

    vn = _layer_norm(v_ref[...], lng_ref[...], lnb_ref[...]).astype(BF16)
    for c in range(tm // CHUNK):
        rows = slice(c * CHUNK, (c + 1) * CHUNK)
        for h in range(H_A):
            cols = slice(h * LANES, (h + 1) * LANES)
            mixed = _dot(ws_ref[h], vn[rows, cols]) + bs_ref[h]
            ya_ref[rows, cols] = (u_ref[rows, cols] * mixed).astype(BF16)


def _even_mix(hcat, seq, ln_g, ln_b, ws, bs, cw, tm=512):
    m = hcat.shape[0]
    w = D_A
    nb8 = m // 8
    r8 = tm // 8
    col = lambda c: pl.BlockSpec((tm, w), lambda i, c=c: (i, c))
    prev = lambda c: pl.BlockSpec((8, w), lambda i, c=c: (jnp.maximum(i * r8 - 1, 0), c))
    nxt = lambda c: pl.BlockSpec((8, w), lambda i, c=c: (jnp.minimum((i + 1) * r8, nb8 - 1), c))
    full = lambda a: pl.BlockSpec(a.shape, lambda i: (0,) * a.ndim)
    return pl.pallas_call(
        functools.partial(_even_mix_kernel, tiles_per_seq=seq // tm),
        out_shape=(jax.ShapeDtypeStruct((m, w), BF16), jax.ShapeDtypeStruct((m, w), BF16)),
        grid=(m // tm,),
        in_specs=[col(0), col(1), col(2), col(3), col(4), prev(3), prev(4), nxt(3), nxt(4),
                  full(ln_g), full(ln_b), full(ws), full(bs), full(cw)],
        out_specs=(pl.BlockSpec((tm, w), lambda i: (i, 0)), pl.BlockSpec((tm, w), lambda i: (i, 0))),
        compiler_params=_params(("parallel",)),
        name="even_mix",
    )(hcat, hcat, hcat, hcat, hcat, hcat, hcat, hcat, hcat, ln_g, ln_b, ws, bs, cw)


def _out_proj_kernel(a1_ref, a2_ref, x_ref, w1_ref, w2_ref, g_ref, b_ref, o_ref):
    mix = _dot(a1_ref[...], w1_ref[...]) + _dot(a2_ref[...], w2_ref[...])
    o_ref[...] = _layer_norm(ALPHA * x_ref[...] + mix, g_ref[...], b_ref[...])


def _out_proj_ln(a1, a2, x, w1, w2, g, b, tm=512):
    m, d = x.shape
    k1, k2 = a1.shape[1], a2.shape[1]
    full = lambda a: pl.BlockSpec(a.shape, lambda i: (0,) * a.ndim)
    return pl.pallas_call(
        _out_proj_kernel,
        out_shape=jax.ShapeDtypeStruct((m, d), F32),
        grid=(m // tm,),
        in_specs=[pl.BlockSpec((tm, k1), lambda i: (i, 0)), pl.BlockSpec((tm, k2), lambda i: (i, 0)),
                  pl.BlockSpec((tm, d), lambda i: (i, 0)), full(w1), full(w2), full(g), full(b)],
        out_specs=pl.BlockSpec((tm, d), lambda i: (i, 0)),
        compiler_params=_params(("parallel",)),
        name="out_proj_ln",
    )(a1, a2, x, w1, w2, g, b)


def _ffn_kernel(x_ref, w1_ref, w2_ref, g_ref, b_ref, o_ref, xb_ref):
    j = pl.program_id(1)

    @pl.when(j == 0)
    def _():
        x = x_ref[...]
        xb_ref[...] = x.astype(BF16)
        o_ref[...] = ALPHA * x

    h = jnp.maximum(_dot(xb_ref[...], w1_ref[...]), 0.0)
    o_ref[...] += _dot((h * h).astype(BF16), w2_ref[...])

    @pl.when(j == pl.num_programs(1) - 1)
    def _():
        o_ref[...] = _layer_norm(o_ref[...], g_ref[...], b_ref[...])


def _ffn_ln(x, w1, w2, g, b, tm=512, tf=512):
    m, d = x.shape
    f = w1.shape[1]
    full = lambda a: pl.BlockSpec(a.shape, lambda i, j: (0,) * a.ndim)
    return pl.pallas_call(
        _ffn_kernel,
        out_shape=jax.ShapeDtypeStruct((m, d), F32),
        grid=(m // tm, f // tf),
        in_specs=[pl.BlockSpec((tm, d), lambda i, j: (i, 0)),
                  pl.BlockSpec((d, tf), lambda i, j: (0, j)),
                  pl.BlockSpec((tf, d), lambda i, j: (j, 0)),
                  full(g), full(b)],
        out_specs=pl.BlockSpec((tm, d), lambda i, j: (i, 0)),
        scratch_shapes=[pltpu.VMEM((tm, d), BF16)],
        compiler_params=_params(("parallel", "arbitrary")),
        name="ffn_ln",
    )(x, w1, w2, g, b)


def _rope(x, cos, sin_signed, is_first_half):
    partner = jnp.where(is_first_half, pltpu.roll(x, LANES - DH_D // 2, axis=1),
                        pltpu.roll(x, DH_D // 2, axis=1))
    return x * cos + partner * sin_signed


def _odd_in_kernel(x_ref, w_ref, cos_ref, sin_ref, xc_ref, qt_ref, k_ref, vt_ref, xb_ref, *, tq):
    tm = x_ref.shape[0]
    j = pl.program_id(1)

    @pl.when(j == 0)
    def _():
        xb_ref[...] = x_ref[...].astype(BF16)

    acc = _dot(xb_ref[...], w_ref[...])
    lane = lax.broadcasted_iota(jnp.int32, (tm, LANES), 1)
    is_first_half = (lane % DH_D) < (DH_D // 2)

    @pl.when(j == 0)
    def _():
        xc_ref[...] = acc

    @pl.when(j == 1)
    def _():
        for h in range(H_D):
            cols = slice(h * LANES, (h + 1) * LANES)
            q = _rope(acc[:, cols], cos_ref[...], sin_ref[...], is_first_half) * Q_SCALE
            for c in range(tm // tq):
                qt_ref[c, cols, :] = q[c * tq:(c + 1) * tq, :].T.astype(BF16)

    @pl.when(j == 2)
    def _():
        for h in range(H_D):
            cols = slice(h * LANES, (h + 1) * LANES)
            k_ref[:, cols] = _rope(acc[:, cols], cos_ref[...], sin_ref[...], is_first_half).astype(BF16)

    @pl.when(j == 3)
    def _():
        pad_row = lax.broadcasted_iota(jnp.int32, (VT_ROWS - DV_D, tm), 0)
        pad = jnp.where(pad_row == 0, 1.0, 0.0).astype(BF16)
        for h in range(H_D):
            cols = slice(h * LANES, (h + 1) * LANES)
            vt_ref[0, h * VT_ROWS:h * VT_ROWS + DV_D, :] = acc[:, cols].T.astype(BF16)
            vt_ref[0, h * VT_ROWS + DV_D:(h + 1) * VT_ROWS, :] = pad


def _odd_in_proj(x, w, cos, sin_signed, seq, tq, tm=512):
    m, d = x.shape
    tn = D_C
    spt = seq // tm
    return pl.pallas_call(
        functools.partial(_odd_in_kernel, tq=tq),
        out_shape=(jax.ShapeDtypeStruct((m, D_C), F32),
                   jax.ShapeDtypeStruct((m // tq, D_D, tq), BF16),
                   jax.ShapeDtypeStruct((m, D_D), BF16),
                   jax.ShapeDtypeStruct((m // tm, H_D * VT_ROWS, tm), BF16)),
        grid=(m // tm, 4),
        in_specs=[pl.BlockSpec((tm, d), lambda i, j: (i, 0)),
                  pl.BlockSpec((d, tn), lambda i, j: (0, j)),
                  pl.BlockSpec((tm, LANES), lambda i, j: (i % spt, 0)),
                  pl.BlockSpec((tm, LANES), lambda i, j: (i % spt, 0))],
        out_specs=(pl.BlockSpec((tm, D_C), lambda i, j: (i, 0)),
                   pl.BlockSpec((tm // tq, D_D, tq), lambda i, j: (i, 0, 0)),
                   pl.BlockSpec((tm, D_D), lambda i, j: (i, 0)),
                   pl.BlockSpec((1, H_D * VT_ROWS, tm), lambda i, j: (i, 0, 0))),
        scratch_shapes=[pltpu.VMEM((tm, d), BF16)],
        compiler_params=_params(("parallel", "arbitrary")),
        name="odd_in_proj",
    )(x, w, cos, sin_signed)


def _fft_stage1_kernel(x_ref, f1h_ref, f1l_ref, wch_ref, wcl_ref, o_ref):
    n1 = x_ref.shape[1]
    groups = x_ref.shape[2] // LANES
    xh, xl = _split(x_ref[0])
    y = _dot3(f1h_ref[...], f1l_ref[...], xh, xl)
    yr = jnp.concatenate([y[:n1, g * LANES:(g + 1) * LANES] for g in range(groups)], axis=0)
    yi = jnp.concatenate([y[n1:, g * LANES:(g + 1) * LANES] for g in range(groups)], axis=0)
    ych, ycl = _split(jnp.concatenate([yr, yi], axis=1))
    u = _dot3(ych, ycl, wch_ref[...], wcl_ref[...])
    for g in range(groups):
        o_ref[0, 0, :, g * LANES:(g + 1) * LANES] = u[g * n1:(g + 1) * n1, :LANES]
        o_ref[0, 1, :, g * LANES:(g + 1) * LANES] = u[g * n1:(g + 1) * n1, LANES:]


def _fft_stage2_kernel(u_ref, f2c_ref, f2s_ref, tc_ref, ts_ref, o_ref, *, scale):
    kb = u_ref.shape[2]
    base = pl.program_id(1) * kb
    f2c = f2c_ref[...]
    f2s = f2s_ref[...]
    for kk in range(kb):
        tc = tc_ref[pl.ds(base + kk, 1), :]
        ts = ts_ref[pl.ds(base + kk, 1), :]
        gc = (f2c * tc - f2s * ts) * scale
        gs = (f2s * tc + f2c * ts) * scale
        gh, gl = _split(jnp.concatenate([gc, gs], axis=1))
        dh, dl = _split(jnp.concatenate([u_ref[0, 0, kk], u_ref[0, 1, kk]], axis=0))
        o_ref[0, :, kk * D_C:(kk + 1) * D_C] = _dot3(gh, gl, dh, dl).astype(BF16)


def _dft_tables(seq):
    n2 = CHUNK
    n1 = seq // n2
    two_pi = 2.0 * np.pi
    a1 = two_pi * np.outer(np.arange(n1), np.arange(n1)) / n1
    f1 = np.concatenate([np.cos(a1), -np.sin(a1)], axis=0)
    ac = two_pi * np.outer(np.arange(GC_W), np.arange(GC_W)) / GC_W
    wc = np.block([[np.cos(ac), -np.sin(ac)], [np.sin(ac), np.cos(ac)]])
    a2 = two_pi * np.outer(np.arange(n2), np.arange(n2)) / n2
    at = two_pi * np.outer(np.arange(n1), np.arange(n2)) / seq

    def split(a):
        a = jnp.asarray(a, F32)
        hi = a.astype(BF16)
        return hi, (a - hi.astype(F32)).astype(BF16)

    f32 = lambda a: jnp.asarray(a, F32)
    return split(f1) + split(wc) + (f32(np.cos(a2)), f32(np.sin(a2)), f32(np.cos(at)), f32(np.sin(at)))


def _fourier_mix(xc, batch, seq, jb=8, kb=4):
    n2 = CHUNK
    n1 = seq // n2
    f1h, f1l, wch, wcl, f2c, f2s, tc, ts = _dft_tables(seq)
    full2 = lambda a: pl.BlockSpec(a.shape, lambda b, i: (0,) * a.ndim)
    wblk = jb * D_C
    u = pl.pallas_call(
        _fft_stage1_kernel,
        out_shape=jax.ShapeDtypeStruct((batch, 2, n1, n2 * D_C), F32),
        grid=(batch, n2 // jb),
        in_specs=[pl.BlockSpec((1, n1, wblk), lambda b, i: (b, 0, i)),
                  full2(f1h), full2(f1l), full2(wch), full2(wcl)],
        out_specs=pl.BlockSpec((1, 2, n1, wblk), lambda b, i: (b, 0, 0, i)),
        compiler_params=_params(("parallel", "parallel")),
        name="fft_stage1",
    )(xc.reshape(batch, n1, n2 * D_C), f1h, f1l, wch, wcl)
    kb = min(kb, n1)
    yc = pl.pallas_call(
        functools.partial(_fft_stage2_kernel, scale=float(1.0 / math.sqrt(seq * GC_W))),
        out_shape=jax.ShapeDtypeStruct((batch, n2, n1 * D_C), BF16),
        grid=(batch, n1 // kb),
        in_specs=[pl.BlockSpec((1, 2, kb, n2, D_C), lambda b, i: (b, 0, i, 0, 0)),
                  full2(f2c), full2(f2s), full2(tc), full2(ts)],
        out_specs=pl.BlockSpec((1, n2, kb * D_C), lambda b, i: (b, 0, i)),
        compiler_params=_params(("parallel", "parallel")),
        name="fft_stage2",
    )(u.reshape(batch, 2, n1, n2, D_C), f2c, f2s, tc, ts)
    return yc.reshape(batch * seq, D_C)


def _attn_kernel(qt_ref, k_ref, vt_ref, lq1_ref, lk1_ref, lq2_ref, lk2_ref, g_ref, o_ref,
                 sa_ref, sb_ref, m_ref, acc_ref, *, lam_init):
    tq = qt_ref.shape[2]
    tk = vt_ref.shape[2]
    nk = vt_ref.shape[0]

    qt = qt_ref[0]
    row = lax.broadcasted_iota(jnp.int32, (LANES, tq), 0)
    zero = jnp.zeros_like(qt)
    q2 = jnp.concatenate([jnp.where(row < DH_D, qt, zero), jnp.where(row >= DH_D, qt, zero)], axis=1)

    m_ref[...] = jnp.full(m_ref.shape, -1e30, F32)
    acc_ref[...] = jnp.zeros(acc_ref.shape, F32)

    def scores(c):
        kc = k_ref[pl.ds(pl.multiple_of(c * tk, tk), tk), :]
        return _dot(kc, q2)

    def accumulate(s_ref, c):
        s = s_ref[...]
        m_old = m_ref[...]
        m_new = jnp.maximum(m_old, jnp.max(s, axis=0, keepdims=True))
        p = jnp.exp2(s - m_new).astype(BF16)
        acc_ref[...] = jnp.exp2(m_old - m_new) * acc_ref[...] + _dot(vt_ref[c], p)
        m_ref[...] = m_new

    sa_ref[...] = scores(0)

    def body(i, carry):
        c = 2 * i
        sb_ref[...] = scores(c + 1)
        accumulate(sa_ref, c)
        sa_ref[...] = scores(c + 2)
        accumulate(sb_ref, c + 1)
        return carry

    lax.fori_loop(0, nk // 2 - 1, body, 0)
    sb_ref[...] = scores(nk - 1)
    accumulate(sa_ref, nk - 2)
    accumulate(sb_ref, nk - 1)

    lam = (jnp.exp(jnp.sum(lq1_ref[...] * lk1_ref[...], axis=1, keepdims=True))
           - jnp.exp(jnp.sum(lq2_ref[...] * lk2_ref[...], axis=1, keepdims=True)) + lam_init)
    on = acc_ref[:DV_D, :] / acc_ref[DV_D:DV_D + 1, :]
    d = on[:, :tq] - lam * on[:, tq:]
    ms = jnp.mean(d * d, axis=0, keepdims=True)
    y = d * lax.rsqrt(ms + LN_EPS) * g_ref[...] * (1.0 - lam_init)
    o_ref[...] = y.T.astype(BF16)


def _diff_attention(qt, k, vt, lq1, lk1, lq2, lk2, g, batch, seq, lam_init):
    tq = qt.shape[2]
    tk = vt.shape[2]
    m = batch * seq
    nq = seq // tq
    gb = jnp.broadcast_to(g.astype(F32).reshape(DV_D, 1), (DV_D, tq))
    vec = lambda a: a.astype(F32).reshape(1, DH_D)
    full = lambda a: pl.BlockSpec(a.shape, lambda b, h, i: (0,) * a.ndim)
    args = (qt, k, vt, vec(lq1), vec(lk1), vec(lq2), vec(lk2), gb)
    return pl.pallas_call(
        functools.partial(_attn_kernel, lam_init=lam_init),
        out_shape=jax.ShapeDtypeStruct((m, D_D), BF16),
        grid=(batch, H_D, nq),
        in_specs=[pl.BlockSpec((1, DV_D, tq), lambda b, h, i: (b * nq + i, h, 0)),
                  pl.BlockSpec((seq, DV_D), lambda b, h, i: (b, h)),
                  pl.BlockSpec((seq // tk, VT_ROWS, tk), lambda b, h, i: (b, h, 0)),
                  full(args[3]), full(args[4]), full(args[5]), full(args[6]), full(gb)],
        out_specs=pl.BlockSpec((tq, DV_D), lambda b, h, i: (b * nq + i, h)),
        scratch_shapes=[pltpu.VMEM((tk, 2 * tq), F32), pltpu.VMEM((tk, 2 * tq), F32),
                        pltpu.VMEM((1, 2 * tq), F32), pltpu.VMEM((VT_ROWS, 2 * tq), F32)],
        compiler_params=_params(("parallel", "parallel", "arbitrary")),
        name="diff_attention",
    )(*args)


def _rope_tables(seq):
    half = DH_D // 2
    inv_freq = 1.0 / (ROPE_THETA ** (jnp.arange(half, dtype=F32) / half))
    ang = jnp.arange(seq, dtype=F32)[:, None] * inv_freq[None, :]
    cos = jnp.tile(jnp.cos(ang), (1, LANES // half))
    sign = jnp.tile(jnp.concatenate([-jnp.ones((half,), F32), jnp.ones((half,), F32)]), LANES // DH_D)
    sin_signed = jnp.tile(jnp.sin(ang), (1, LANES // half)) * sign[None, :]
    return cos, sin_signed


def _trunk(x3, p):
    batch, seq, d = x3.shape
    x = x3.reshape(batch * seq, d)
    row = lambda a: a.astype(F32).reshape(1, -1)
    for l in range(DEPTH):
        i = l // 2
        if l % 2 == 0:
            hcat = _even_in_proj(x, p["w_in_ab"][i])
            bs = jnp.broadcast_to(p["sgu_b"][i].astype(F32)[:, :, None], (H_A, CHUNK, LANES))
            ya, yb = _even_mix(hcat, seq, row(p["sgu_ln_g"][i]), row(p["sgu_ln_b"][i]),
                               p["sgu_w"][i], bs, p["conv_w"][i].astype(F32))
            w_out = p["w_out_ab"][i]
            x = _out_proj_ln(ya, yb, x, w_out[:D_A], w_out[D_A:], row(p["ln1_g"][l]), row(p["ln1_b"][l]))
        else:
            lam_init = 0.8 - 0.6 * math.exp(-0.3 * l)
            cos, sin_signed = _rope_tables(seq)
            xc, qt, k, vt = _odd_in_proj(x, p["w_in_cd"][i], cos, sin_signed, seq, tq=256)
            yc = _fourier_mix(xc, batch, seq)
            yd = _diff_attention(qt, k, vt, p["lambda_q1"][i], p["lambda_k1"][i], p["lambda_q2"][i],
                                 p["lambda_k2"][i], p["subln_g"][i], batch, seq, lam_init)
            w_out = p["w_out_cd"][i]
            x = _out_proj_ln(yc, yd, x, w_out[:D_C], w_out[D_C:], row(p["ln1_g"][l]), row(p["ln1_b"][l]))
        x = _ffn_ln(x, p["w_ff1"][l], p["w_ff2"][l], row(p["ln2_g"][l]), row(p["ln2_b"][l]))
    return x.reshape(batch, seq, d)


def kernel(x_prompt, x_sample, w_in_ab, sgu_ln_g, sgu_ln_b, sgu_w, sgu_b, conv_w, w_out_ab, w_in_cd,
           lambda_q1, lambda_k1, lambda_q2, lambda_k2, subln_g, w_out_cd, ln1_g, ln1_b, ln2_g, ln2_b,
           w_ff1, w_ff2):
    p = dict(w_in_ab=w_in_ab.astype(BF16), sgu_ln_g=sgu_ln_g, sgu_ln_b=sgu_ln_b, sgu_w=sgu_w.astype(BF16),
             sgu_b=sgu_b, conv_w=conv_w, w_out_ab=w_out_ab.astype(BF16), w_in_cd=w_in_cd.astype(BF16),
             lambda_q1=lambda_q1, lambda_k1=lambda_k1, lambda_q2=lambda_q2, lambda_k2=lambda_k2,
             subln_g=subln_g, w_out_cd=w_out_cd.astype(BF16), ln1_g=ln1_g, ln1_b=ln1_b, ln2_g=ln2_g,
             ln2_b=ln2_b, w_ff1=w_ff1.astype(BF16), w_ff2=w_ff2.astype(BF16))
    return (_trunk(x_prompt, p), _trunk(x_sample, p))
```

```python
import functools
import math

import numpy as np
import jax
import jax.numpy as jnp
from jax import lax
from jax.experimental import pallas as pl
from jax.experimental.pallas import tpu as pltpu

D_MODEL = 2048
DEPTH = 2
CHUNK = 128
D_A = D_MODEL // 2
H_A = 8
D_B = D_MODEL // 2
D_C = D_MODEL // 2
GC_W = 128
H_D = 8
DH_D = 64
DV_D = 2 * DH_D
D_D = H_D * DV_D
D_FF = 4 * D_MODEL
ROPE_THETA = 10000.0
LN_EPS = 1e-5
ALPHA = (2 * DEPTH) ** 0.25
Q_SCALE = DH_D ** -0.5 * math.log2(math.e)
VT_ROWS = DV_D + 16
ATTN_UNROLL = 4

LANES = 128
VMEM_LIMIT_BYTES = 56 * 1024 * 1024

BF16 = jnp.bfloat16
F32 = jnp.float32


def _params(semantics):
    return pltpu.CompilerParams(dimension_semantics=semantics,
                                vmem_limit_bytes=VMEM_LIMIT_BYTES)


def _dot(a, b):
    return jnp.dot(a, b, preferred_element_type=F32)


def _split(x):
    hi = x.astype(BF16)
    lo = (x - hi.astype(F32)).astype(BF16)
    return hi, lo


def _dot3(a_hi, a_lo, b_hi, b_lo):
    return _dot(a_hi, b_hi) + (_dot(a_lo, b_hi) + _dot(a_hi, b_lo))


def _layer_norm(x, g, b):
    mu = jnp.mean(x, axis=-1, keepdims=True)
    xc = x - mu
    var = jnp.mean(xc * xc, axis=-1, keepdims=True)
    return xc * lax.rsqrt(var + LN_EPS) * g + b


def _even_in_kernel(x_ref, w_ref, o_ref, xb_ref):
    j = pl.program_id(1)

    @pl.when(j == 0)
    def _():
        xb_ref[...] = x_ref[...].astype(BF16)

    acc = _dot(xb_ref[...], w_ref[...])

    @pl.when(j < 2)
    def _():
        o_ref[...] = jax.nn.gelu(acc)

    @pl.when(j >= 2)
    def _():
        o_ref[...] = acc


def _even_in_proj(x, w, tm=1024, tn=1024):
    m, d = x.shape
    n = w.shape[1]
    return pl.pallas_call(
        _even_in_kernel,
        out_shape=jax.ShapeDtypeStruct((m, n), F32),
        grid=(m // tm, n // tn),
        in_specs=[pl.BlockSpec((tm, d), lambda i, j: (i, 0)),
                  pl.BlockSpec((d, tn), lambda i, j: (0, j))],
        out_specs=pl.BlockSpec((tm, tn), lambda i, j: (i, j)),
        scratch_shapes=[pltpu.VMEM((tm, d), BF16)],
        compiler_params=_params(("parallel", "arbitrary")),
        name="even_in_proj",
    )(x, w)


def _even_mix_kernel(u_ref, v_ref, bg_ref, cg_ref, xp_ref, cgp_ref, xpp_ref, cgn_ref, xpn_ref,
                     lng_ref, lnb_ref, ws_ref, bs_ref, cw_ref, ya_ref, yb_ref, *, tiles_per_seq):
    tm = u_ref.shape[0]
    i = pl.program_id(0)
    t = i % tiles_per_seq
    not_first = (t != 0).astype(F32)
    not_last = (t != tiles_per_seq - 1).astype(F32)

    z = cg_ref[...] * xp_ref[...]
    z_before = cgp_ref[7:8, :] * xpp_ref[7:8, :] * not_first
    z_after = cgn_ref[0:1, :] * xpn_ref[0:1, :] * not_last
    row = lax.broadcasted_iota(jnp.int32, (tm, 1), 0)
    z_dn = jnp.where(row == 0, z_before, pltpu.roll(z, 1, axis=0))
    z_up = jnp.where(row == tm - 1, z_after, pltpu.roll(z, tm - 1, axis=0))
    conv = cw_ref[0:1, :] * z_dn + cw_ref[1:2, :] * z + cw_ref[2:3, :] * z_up
    yb_ref[...] = (bg_ref[...] * conv).astype(BF16)

    vn = _layer_norm(v_ref[...], lng_ref[...], lnb_ref[...]).astype(BF16)
    for c in range(tm // CHUNK):
        rows = slice(c * CHUNK, (c + 1) * CHUNK)
        for h in range(H_A):
            cols = slice(h * LANES, (h + 1) * LANES)
            mixed = _dot(ws_ref[h], vn[rows, cols]) + bs_ref[h]
            ya_ref[rows, cols] = (u_ref[rows, cols] * mixed).astype(BF16)


def _even_mix(hcat, seq, ln_g, ln_b, ws, bs, cw, tm=512):
    m = hcat.shape[0]
    w = D_A
    nb8 = m // 8
    r8 = tm // 8
    col = lambda c: pl.BlockSpec((tm, w), lambda i, c=c: (i, c))
    prev = lambda c: pl.BlockSpec((8, w), lambda i, c=c: (jnp.maximum(i * r8 - 1, 0), c))
    nxt = lambda c: pl.BlockSpec((8, w), lambda i, c=c: (jnp.minimum((i + 1) * r8, nb8 - 1), c))
    full = lambda a: pl.BlockSpec(a.shape, lambda i: (0,) * a.ndim)
    return pl.pallas_call(
        functools.partial(_even_mix_kernel, tiles_per_seq=seq // tm),
        out_shape=(jax.ShapeDtypeStruct((m, w), BF16), jax.ShapeDtypeStruct((m, w), BF16)),
        grid=(m // tm,),
        in_specs=[col(0), col(1), col(2), col(3), col(4), prev(3), prev(4), nxt(3), nxt(4),
                  full(ln_g), full(ln_b), full(ws), full(bs), full(cw)],
        out_specs=(pl.BlockSpec((tm, w), lambda i: (i, 0)), pl.BlockSpec((tm, w), lambda i: (i, 0))),
        compiler_params=_params(("parallel",)),
        name="even_mix",
    )(hcat, hcat, hcat, hcat, hcat, hcat, hcat, hcat, hcat, ln_g, ln_b, ws, bs, cw)


def _out_proj_kernel(a1_ref, a2_ref, x_ref, w1_ref, w2_ref, g_ref, b_ref, o_ref):
    mix = _dot(a1_ref[...], w1_ref[...]) + _dot(a2_ref[...], w2_ref[...])
    o_ref[...] = _layer_norm(ALPHA * x_ref[...] + mix, g_ref[...], b_ref[...])


def _out_proj_ln(a1, a2, x, w1, w2, g, b, tm=512):
    m, d = x.shape
    k1, k2 = a1.shape[1], a2.shape[1]
    full = lambda a: pl.BlockSpec(a.shape, lambda i: (0,) * a.ndim)
    return pl.pallas_call(
        _out_proj_kernel,
        out_shape=jax.ShapeDtypeStruct((m, d), F32),
        grid=(m // tm,),
        in_specs=[pl.BlockSpec((tm, k1), lambda i: (i, 0)), pl.BlockSpec((tm, k2), lambda i: (i, 0)),
                  pl.BlockSpec((tm, d), lambda i: (i, 0)), full(w1), full(w2), full(g), full(b)],
        out_specs=pl.BlockSpec((tm, d), lambda i: (i, 0)),
        compiler_params=_params(("parallel",)),
        name="out_proj_ln",
    )(a1, a2, x, w1, w2, g, b)


def _ffn_kernel(x_ref, w1_ref, w2_ref, g_ref, b_ref, o_ref, xb_ref):
    j = pl.program_id(1)

    @pl.when(j == 0)
    def _():
        x = x_ref[...]
        xb_ref[...] = x.astype(BF16)
        o_ref[...] = ALPHA * x

    h = jnp.maximum(_dot(xb_ref[...], w1_ref[...]), 0.0)
    o_ref[...] += _dot((h * h).astype(BF16), w2_ref[...])

    @pl.when(j == pl.num_programs(1) - 1)
    def _():
        o_ref[...] = _layer_norm(o_ref[...], g_ref[...], b_ref[...])


def _ffn_ln(x, w1, w2, g, b, tm=512, tf=1024):
    m, d = x.shape
    f = w1.shape[1]
    full = lambda a: pl.BlockSpec(a.shape, lambda i, j: (0,) * a.ndim)
    return pl.pallas_call(
        _ffn_kernel,
        out_shape=jax.ShapeDtypeStruct((m, d), F32),
        grid=(m // tm, f // tf),
        in_specs=[pl.BlockSpec((tm, d), lambda i, j: (i, 0)),
                  pl.BlockSpec((d, tf), lambda i, j: (0, j)),
                  pl.BlockSpec((tf, d), lambda i, j: (j, 0)),
                  full(g), full(b)],
        out_specs=pl.BlockSpec((tm, d), lambda i, j: (i, 0)),
        scratch_shapes=[pltpu.VMEM((tm, d), BF16)],
        compiler_params=_params(("parallel", "arbitrary")),
        name="ffn_ln",
    )(x, w1, w2, g, b)


def _rope(x, cos, sin_signed, is_first_half):
    partner = jnp.where(is_first_half, pltpu.roll(x, LANES - DH_D // 2, axis=1),
                        pltpu.roll(x, DH_D // 2, axis=1))
    return x * cos + partner * sin_signed


def _odd_in_kernel(x_ref, w_ref, cos_ref, sin_ref, xc_ref, qt_ref, k_ref, vt_ref, xb_ref, *, tq):
    tm = x_ref.shape[0]
    j = pl.program_id(1)

    @pl.when(j == 0)
    def _():
        xb_ref[...] = x_ref[...].astype(BF16)

    acc = _dot(xb_ref[...], w_ref[...])
    lane = lax.broadcasted_iota(jnp.int32, (tm, LANES), 1)
    is_first_half = (lane % DH_D) < (DH_D // 2)

    @pl.when(j == 0)
    def _():
        xc_ref[...] = acc

    @pl.when(j == 1)
    def _():
        for h in range(H_D):
            cols = slice(h * LANES, (h + 1) * LANES)
            q = _rope(acc[:, cols], cos_ref[...], sin_ref[...], is_first_half) * Q_SCALE
            for c in range(tm // tq):
                qt_ref[c, cols, :] = q[c * tq:(c + 1) * tq, :].T.astype(BF16)

    @pl.when(j == 2)
    def _():
        for h in range(H_D):
            cols = slice(h * LANES, (h + 1) * LANES)
            k_ref[:, cols] = _rope(acc[:, cols], cos_ref[...], sin_ref[...], is_first_half).astype(BF16)

    @pl.when(j == 3)
    def _():
        pad_row = lax.broadcasted_iota(jnp.int32, (VT_ROWS - DV_D, tm), 0)
        pad = jnp.where(pad_row == 0, 1.0, 0.0).astype(BF16)
        for h in range(H_D):
            cols = slice(h * LANES, (h + 1) * LANES)
            vt_ref[0, h * VT_ROWS:h * VT_ROWS + DV_D, :] = acc[:, cols].T.astype(BF16)
            vt_ref[0, h * VT_ROWS + DV_D:(h + 1) * VT_ROWS, :] = pad


def _odd_in_proj(x, w, cos, sin_signed, seq, tq, tm=512):
    m, d = x.shape
    tn = D_C
    spt = seq // tm
    return pl.pallas_call(
        functools.partial(_odd_in_kernel, tq=tq),
        out_shape=(jax.ShapeDtypeStruct((m, D_C), F32),
                   jax.ShapeDtypeStruct((m // tq, D_D, tq), BF16),
                   jax.ShapeDtypeStruct((m, D_D), BF16),
                   jax.ShapeDtypeStruct((m // tm, H_D * VT_ROWS, tm), BF16)),
        grid=(m // tm, 4),
        in_specs=[pl.BlockSpec((tm, d), lambda i, j: (i, 0)),
                  pl.BlockSpec((d, tn), lambda i, j: (0, j)),
                  pl.BlockSpec((tm, LANES), lambda i, j: (i % spt, 0)),
                  pl.BlockSpec((tm, LANES), lambda i, j: (i % spt, 0))],
        out_specs=(pl.BlockSpec((tm, D_C), lambda i, j: (i, 0)),
                   pl.BlockSpec((tm // tq, D_D, tq), lambda i, j: (i, 0, 0)),
                   pl.BlockSpec((tm, D_D), lambda i, j: (i, 0)),
                   pl.BlockSpec((1, H_D * VT_ROWS, tm), lambda i, j: (i, 0, 0))),
        scratch_shapes=[pltpu.VMEM((tm, d), BF16)],
        compiler_params=_params(("parallel", "arbitrary")),
        name="odd_in_proj",
    )(x, w, cos, sin_signed)


def _fft_stage1_kernel(x_ref, f1h_ref, f1l_ref, wch_ref, wcl_ref, o_ref):
    n1 = x_ref.shape[1]
    groups = x_ref.shape[2] // LANES
    xh, xl = _split(x_ref[0])
    y = _dot3(f1h_ref[...], f1l_ref[...], xh, xl)
    yr = jnp.concatenate([y[:n1, g * LANES:(g + 1) * LANES] for g in range(groups)], axis=0)
    yi = jnp.concatenate([y[n1:, g * LANES:(g + 1) * LANES] for g in range(groups)], axis=0)
    ych, ycl = _split(jnp.concatenate([yr, yi], axis=1))
    u = _dot3(ych, ycl, wch_ref[...], wcl_ref[...])
    for g in range(groups):
        o_ref[0, 0, :, g * LANES:(g + 1) * LANES] = u[g * n1:(g + 1) * n1, :LANES]
        o_ref[0, 1, :, g * LANES:(g + 1) * LANES] = u[g * n1:(g + 1) * n1, LANES:]


def _fft_stage2_kernel(u_ref, f2c_ref, f2s_ref, tc_ref, ts_ref, o_ref, *, scale):
    kb = u_ref.shape[2]
    base = pl.program_id(1) * kb
    f2c = f2c_ref[...]
    f2s = f2s_ref[...]
    for kk in range(kb):
        tc = tc_ref[pl.ds(base + kk, 1), :]
        ts = ts_ref[pl.ds(base + kk, 1), :]
        gc = (f2c * tc - f2s * ts) * scale
        gs = (f2s * tc + f2c * ts) * scale
        gh, gl = _split(jnp.concatenate([gc, gs], axis=1))
        dh, dl = _split(jnp.concatenate([u_ref[0, 0, kk], u_ref[0, 1, kk]], axis=0))
        o_ref[0, :, kk * D_C:(kk + 1) * D_C] = _dot3(gh, gl, dh, dl).astype(BF16)


def _dft_tables(seq):
    n2 = CHUNK
    n1 = seq // n2
    two_pi = 2.0 * np.pi
    a1 = two_pi * np.outer(np.arange(n1), np.arange(n1)) / n1
    f1 = np.concatenate([np.cos(a1), -np.sin(a1)], axis=0)
    ac = two_pi * np.outer(np.arange(GC_W), np.arange(GC_W)) / GC_W
    wc = np.block([[np.cos(ac), -np.sin(ac)], [np.sin(ac), np.cos(ac)]])
    a2 = two_pi * np.outer(np.arange(n2), np.arange(n2)) / n2
    at = two_pi * np.outer(np.arange(n1), np.arange(n2)) / seq

    def split(a):
        a = jnp.asarray(a, F32)
        hi = a.astype(BF16)
        return hi, (a - hi.astype(F32)).astype(BF16)

    f32 = lambda a: jnp.asarray(a, F32)
    return split(f1) + split(wc) + (f32(np.cos(a2)), f32(np.sin(a2)), f32(np.cos(at)), f32(np.sin(at)))


def _fourier_mix(xc, batch, seq, jb=8, kb=4):
    n2 = CHUNK
    n1 = seq // n2
    f1h, f1l, wch, wcl, f2c, f2s, tc, ts = _dft_tables(seq)
    full2 = lambda a: pl.BlockSpec(a.shape, lambda b, i: (0,) * a.ndim)
    wblk = jb * D_C
    u = pl.pallas_call(
        _fft_stage1_kernel,
        out_shape=jax.ShapeDtypeStruct((batch, 2, n1, n2 * D_C), F32),
        grid=(batch, n2 // jb),
        in_specs=[pl.BlockSpec((1, n1, wblk), lambda b, i: (b, 0, i)),
                  full2(f1h), full2(f1l), full2(wch), full2(wcl)],
        out_specs=pl.BlockSpec((1, 2, n1, wblk), lambda b, i: (b, 0, 0, i)),
        compiler_params=_params(("parallel", "parallel")),
        name="fft_stage1",
    )(xc.reshape(batch, n1, n2 * D_C), f1h, f1l, wch, wcl)
    kb = min(kb, n1)
    yc = pl.pallas_call(
        functools.partial(_fft_stage2_kernel, scale=float(1.0 / math.sqrt(seq * GC_W))),
        out_shape=jax.ShapeDtypeStruct((batch, n2, n1 * D_C), BF16),
        grid=(batch, n1 // kb),
        in_specs=[pl.BlockSpec((1, 2, kb, n2, D_C), lambda b, i: (b, 0, i, 0, 0)),
                  full2(f2c), full2(f2s), full2(tc), full2(ts)],
        out_specs=pl.BlockSpec((1, n2, kb * D_C), lambda b, i: (b, 0, i)),
        compiler_params=_params(("parallel", "parallel")),
        name="fft_stage2",
    )(u.reshape(batch, 2, n1, n2, D_C), f2c, f2s, tc, ts)
    return yc.reshape(batch * seq, D_C)


def _attn_kernel(qt_ref, k_ref, vt_ref, lq1_ref, lk1_ref, lq2_ref, lk2_ref, g_ref, o_ref,
                 sa_ref, sb_ref, m_ref, acc_ref, *, lam_init):
    tq = qt_ref.shape[2]
    tk = vt_ref.shape[2]
    nk = vt_ref.shape[0]

    qt = qt_ref[0]
    row = lax.broadcasted_iota(jnp.int32, (LANES, tq), 0)
    zero = jnp.zeros_like(qt)
    q2 = jnp.concatenate([jnp.where(row < DH_D, qt, zero), jnp.where(row >= DH_D, qt, zero)], axis=1)

    m_ref[...] = jnp.full(m_ref.shape, -1e30, F32)
    acc_ref[...] = jnp.zeros(acc_ref.shape, F32)

    def scores(c):
        kc = k_ref[pl.ds(pl.multiple_of(c * tk, tk), tk), :]
        return _dot(kc, q2)

    def accumulate(s_ref, c):
        s = s_ref[...]
        m_old = m_ref[...]
        m_new = jnp.maximum(m_old, jnp.max(s, axis=0, keepdims=True))
        p = jnp.exp2(s - m_new).astype(BF16)
        acc_ref[...] = jnp.exp2(m_old - m_new) * acc_ref[...] + _dot(vt_ref[c], p)
        m_ref[...] = m_new

    bufs = (sa_ref, sb_ref)
    sa_ref[...] = scores(0)

    unroll = ATTN_UNROLL if nk >= 4 * ATTN_UNROLL else 2

    def run_chunks(c0, feed_next):
        for u in range(unroll):
            if u + 1 < unroll or feed_next:
                bufs[(u + 1) % 2][...] = scores(c0 + u + 1)
            accumulate(bufs[u % 2], c0 + u)

    def body(i, carry):
        run_chunks(unroll * i, True)
        return carry

    lax.fori_loop(0, nk // unroll - 1, body, 0)
    run_chunks(nk - unroll, False)

    lam = (jnp.exp(jnp.sum(lq1_ref[...] * lk1_ref[...], axis=1, keepdims=True))
           - jnp.exp(jnp.sum(lq2_ref[...] * lk2_ref[...], axis=1, keepdims=True)) + lam_init)
    on = acc_ref[:DV_D, :] / acc_ref[DV_D:DV_D + 1, :]
    d = on[:, :tq] - lam * on[:, tq:]
    ms = jnp.mean(d * d, axis=0, keepdims=True)
    y = d * lax.rsqrt(ms + LN_EPS) * g_ref[...] * (1.0 - lam_init)
    o_ref[...] = y.T.astype(BF16)


def _diff_attention(qt, k, vt, lq1, lk1, lq2, lk2, g, batch, seq, lam_init):
    tq = qt.shape[2]
    tk = vt.shape[2]
    m = batch * seq
    nq = seq // tq
    gb = jnp.broadcast_to(g.astype(F32).reshape(DV_D, 1), (DV_D, tq))
    vec = lambda a: a.astype(F32).reshape(1, DH_D)
    full = lambda a: pl.BlockSpec(a.shape, lambda b, h, i: (0,) * a.ndim)
    args = (qt, k, vt, vec(lq1), vec(lk1), vec(lq2), vec(lk2), gb)
    return pl.pallas_call(
        functools.partial(_attn_kernel, lam_init=lam_init),
        out_shape=jax.ShapeDtypeStruct((m, D_D), BF16),
        grid=(batch, H_D, nq),
        in_specs=[pl.BlockSpec((1, DV_D, tq), lambda b, h, i: (b * nq + i, h, 0)),
                  pl.BlockSpec((seq, DV_D), lambda b, h, i: (b, h)),
                  pl.BlockSpec((seq // tk, VT_ROWS, tk), lambda b, h, i: (b, h, 0)),
                  full(args[3]), full(args[4]), full(args[5]), full(args[6]), full(gb)],
        out_specs=pl.BlockSpec((tq, DV_D), lambda b, h, i: (b * nq + i, h)),
        scratch_shapes=[pltpu.VMEM((tk, 2 * tq), F32), pltpu.VMEM((tk, 2 * tq), F32),
                        pltpu.VMEM((1, 2 * tq), F32), pltpu.VMEM((VT_ROWS, 2 * tq), F32)],
        compiler_params=_params(("parallel", "parallel", "arbitrary")),
        name="diff_attention",
    )(*args)


def _rope_tables(seq):
    half = DH_D // 2
    inv_freq = 1.0 / (ROPE_THETA ** (jnp.arange(half, dtype=F32) / half))
    ang = jnp.arange(seq, dtype=F32)[:, None] * inv_freq[None, :]
    cos = jnp.tile(jnp.cos(ang), (1, LANES // half))
    sign = jnp.tile(jnp.concatenate([-jnp.ones((half,), F32), jnp.ones((half,), F32)]), LANES // DH_D)
    sin_signed = jnp.tile(jnp.sin(ang), (1, LANES // half)) * sign[None, :]
    return cos, sin_signed


def _trunk(x3, p):
    batch, seq, d = x3.shape
    x = x3.reshape(batch * seq, d)
    row = lambda a: a.astype(F32).reshape(1, -1)
    for l in range(DEPTH):
        i = l // 2
        if l % 2 == 0:
            hcat = _even_in_proj(x, p["w_in_ab"][i])
            bs = jnp.broadcast_to(p["sgu_b"][i].astype(F32)[:, :, None], (H_A, CHUNK, LANES))
            ya, yb = _even_mix(hcat, seq, row(p["sgu_ln_g"][i]), row(p["sgu_ln_b"][i]),
                               p["sgu_w"][i], bs, p["conv_w"][i].astype(F32))
            w_out = p["w_out_ab"][i]
            x = _out_proj_ln(ya, yb, x, w_out[:D_A], w_out[D_A:], row(p["ln1_g"][l]), row(p["ln1_b"][l]))
        else:
            lam_init = 0.8 - 0.6 * math.exp(-0.3 * l)
            cos, sin_signed = _rope_tables(seq)
            xc, qt, k, vt = _odd_in_proj(x, p["w_in_cd"][i], cos, sin_signed, seq, tq=256)
            yc = _fourier_mix(xc, batch, seq)
            yd = _diff_attention(qt, k, vt, p["lambda_q1"][i], p["lambda_k1"][i], p["lambda_q2"][i],
                                 p["lambda_k2"][i], p["subln_g"][i], batch, seq, lam_init)
            w_out = p["w_out_cd"][i]
            x = _out_proj_ln(yc, yd, x, w_out[:D_C], w_out[D_C:], row(p["ln1_g"][l]), row(p["ln1_b"][l]))
        x = _ffn_ln(x, p["w_ff1"][l], p["w_ff2"][l], row(p["ln2_g"][l]), row(p["ln2_b"][l]))
    return x.reshape(batch, seq, d)


def kernel(x_prompt, x_sample, w_in_ab, sgu_ln_g, sgu_ln_b, sgu_w, sgu_b, conv_w, w_out_ab, w_in_cd,
           lambda_q1, lambda_k1, lambda_q2, lambda_k2, subln_g, w_out_cd, ln1_g, ln1_b, ln2_g, ln2_b,
           w_ff1, w_ff2):
    p = dict(w_in_ab=w_in_ab.astype(BF16), sgu_ln_g=sgu_ln_g, sgu_ln_b=sgu_ln_b, sgu_w=sgu_w.astype(BF16),
             sgu_b=sgu_b, conv_w=conv_w, w_out_ab=w_out_ab.astype(BF16), w_in_cd=w_in_cd.astype(BF16),
             lambda_q1=lambda_q1, lambda_k1=lambda_k1, lambda_q2=lambda_q2, lambda_k2=lambda_k2,
             subln_g=subln_g, w_out_cd=w_out_cd.astype(BF16), ln1_g=ln1_g, ln1_b=ln1_b, ln2_g=ln2_g,
             ln2_b=ln2_b, w_ff1=w_ff1.astype(BF16), w_ff2=w_ff2.astype(BF16))
    return (_trunk(x_prompt, p), _trunk(x_sample, p))
```

```python
import functools
import math

import numpy as np
import jax
import jax.numpy as jnp
from jax import lax
from jax.experimental import pallas as pl
from jax.experimental.pallas import tpu as pltpu

D_MODEL = 2048
DEPTH = 2
CHUNK = 128
D_A = D_MODEL // 2
H_A = 8
D_B = D_MODEL // 2
D_C = D_MODEL // 2
GC_W = 128
H_D = 8
DH_D = 64
DV_D = 2 * DH_D
D_D = H_D * DV_D
D_FF = 4 * D_MODEL
ROPE_THETA = 10000.0
LN_EPS = 1e-5
ALPHA = (2 * DEPTH) ** 0.25
Q_SCALE = DH_D ** -0.5 * math.log2(math.e)
VT_ROWS = DV_D + 16
ATTN_UNROLL = 4

LANES = 128
VMEM_LIMIT_BYTES = 56 * 1024 * 1024

BF16 = jnp.bfloat16
F32 = jnp.float32


def _params(semantics):
    return pltpu.CompilerParams(dimension_semantics=semantics,
                                vmem_limit_bytes=VMEM_LIMIT_BYTES)


def _dot(a, b):
    return jnp.dot(a, b, preferred_element_type=F32)


def _split(x):
    hi = x.astype(BF16)
    lo = (x - hi.astype(F32)).astype(BF16)
    return hi, lo


def _dot3(a_hi, a_lo, b_hi, b_lo):
    return _dot(a_hi, b_hi) + (_dot(a_lo, b_hi) + _dot(a_hi, b_lo))


def _layer_norm(x, g, b):
    mu = jnp.mean(x, axis=-1, keepdims=True)
    xc = x - mu
    var = jnp.mean(xc * xc, axis=-1, keepdims=True)
    return xc * lax.rsqrt(var + LN_EPS) * g + b


def _even_in_kernel(x_ref, w_ref, o_ref, xb_ref):
    j = pl.program_id(1)

    @pl.when(j == 0)
    def _():
        xb_ref[...] = x_ref[...].astype(BF16)

    acc = _dot(xb_ref[...], w_ref[...])

    @pl.when(j < 2)
    def _():
        o_ref[...] = jax.nn.gelu(acc)

    @pl.when(j >= 2)
    def _():
        o_ref[...] = acc


def _even_in_proj(x, w, tm=1024, tn=1024):
    m, d = x.shape
    n = w.shape[1]
    return pl.pallas_call(
        _even_in_kernel,
        out_shape=jax.ShapeDtypeStruct((m, n), F32),
        grid=(m // tm, n // tn),
        in_specs=[pl.BlockSpec((tm, d), lambda i, j: (i, 0)),
                  pl.BlockSpec((d, tn), lambda i, j: (0, j))],
        out_specs=pl.BlockSpec((tm, tn), lambda i, j: (i, j)),
        scratch_shapes=[pltpu.VMEM((tm, d), BF16)],
        compiler_params=_params(("parallel", "arbitrary")),
        name="even_in_proj",
    )(x, w)


def _even_mix_kernel(u_ref, v_ref, bg_ref, cg_ref, xp_ref, cgp_ref, xpp_ref, cgn_ref, xpn_ref,
                     lng_ref, lnb_ref, ws_ref, bs_ref, cw_ref, ya_ref, yb_ref, *, tiles_per_seq):
    tm = u_ref.shape[0]
    i = pl.program_id(0)
    t = i % tiles_per_seq
    not_first = (t != 0).astype(F32)
    not_last = (t != tiles_per_seq - 1).astype(F32)

    z = cg_ref[...] * xp_ref[...]
    z_before = cgp_ref[7:8, :] * xpp_ref[7:8, :] * not_first
    z_after = cgn_ref[0:1, :] * xpn_ref[0:1, :] * not_last
    row = lax.broadcasted_iota(jnp.int32, (tm, 1), 0)
    z_dn = jnp.where(row == 0, z_before, pltpu.roll(z, 1, axis=0))
    z_up = jnp.where(row == tm - 1, z_after, pltpu.roll(z, tm - 1, axis=0))
    conv = cw_ref[0:1, :] * z_dn + cw_ref[1:2, :] * z + cw_ref[2:3, :] * z_up
    yb_ref[...] = (bg_ref[...] * conv).astype(BF16)

    vn = _layer_norm(v_ref[...], lng_ref[...], lnb_ref[...]).astype(BF16)
    for c in range(tm // CHUNK):
        rows = slice(c * CHUNK, (c + 1) * CHUNK)
        for h in range(H_A):
            cols = slice(h * LANES, (h + 1) * LANES)
            mixed = _dot(ws_ref[h], vn[rows, cols]) + bs_ref[h]
            ya_ref[rows, cols] = (u_ref[rows, cols] * mixed).astype(BF16)


def _even_mix(hcat, seq, ln_g, ln_b, ws, bs, cw, tm=512):
    m = hcat.shape[0]
    w = D_A
    nb8 = m // 8
    r8 = tm // 8
    col = lambda c: pl.BlockSpec((tm, w), lambda i, c=c: (i, c))
    prev = lambda c: pl.BlockSpec((8, w), lambda i, c=c: (jnp.maximum(i * r8 - 1, 0), c))
    nxt = lambda c: pl.BlockSpec((8, w), lambda i, c=c: (jnp.minimum((i + 1) * r8, nb8 - 1), c))
    full = lambda a: pl.BlockSpec(a.shape, lambda i: (0,) * a.ndim)
    return pl.pallas_call(
        functools.partial(_even_mix_kernel, tiles_per_seq=seq // tm),
        out_shape=(jax.ShapeDtypeStruct((m, w), BF16), jax.ShapeDtypeStruct((m, w), BF16)),
        grid=(m // tm,),
        in_specs=[col(0), col(1), col(2), col(3), col(4), prev(3), prev(4), nxt(3), nxt(4),
                  full(ln_g), full(ln_b), full(ws), full(bs), full(cw)],
        out_specs=(pl.BlockSpec((tm, w), lambda i: (i, 0)), pl.BlockSpec((tm, w), lambda i: (i, 0))),
        compiler_params=_params(("parallel",)),
        name="even_mix",
    )(hcat, hcat, hcat, hcat, hcat, hcat, hcat, hcat, hcat, ln_g, ln_b, ws, bs, cw)


def _out_proj_kernel(a1_ref, a2_ref, x_ref, w1_ref, w2_ref, g_ref, b_ref, o_ref):
    mix = _dot(a1_ref[...], w1_ref[...]) + _dot(a2_ref[...], w2_ref[...])
    o_ref[...] = _layer_norm(ALPHA * x_ref[...] + mix, g_ref[...], b_ref[...])


def _out_proj_ln(a1, a2, x, w1, w2, g, b, tm=512):
    m, d = x.shape
    k1, k2 = a1.shape[1], a2.shape[1]
    full = lambda a: pl.BlockSpec(a.shape, lambda i: (0,) * a.ndim)
    return pl.pallas_call(
        _out_proj_kernel,
        out_shape=jax.ShapeDtypeStruct((m, d), F32),
        grid=(m // tm,),
        in_specs=[pl.BlockSpec((tm, k1), lambda i: (i, 0)), pl.BlockSpec((tm, k2), lambda i: (i, 0)),
                  pl.BlockSpec((tm, d), lambda i: (i, 0)), full(w1), full(w2), full(g), full(b)],
        out_specs=pl.BlockSpec((tm, d), lambda i: (i, 0)),
        compiler_params=_params(("parallel",)),
        name="out_proj_ln",
    )(a1, a2, x, w1, w2, g, b)


def _ffn_kernel(x_ref, w1_ref, w2_ref, g_ref, b_ref, o_ref, *rest):
    xb_ref = rest[-1]
    j = pl.program_id(1)

    @pl.when(j == 0)
    def _():
        x = x_ref[...]
        xb_ref[...] = x.astype(BF16)
        o_ref[...] = ALPHA * x

    h = jnp.maximum(_dot(xb_ref[...], w1_ref[...]), 0.0)
    o_ref[...] += _dot((h * h).astype(BF16), w2_ref[...])

    @pl.when(j == pl.num_programs(1) - 1)
    def _():
        y = _layer_norm(o_ref[...], g_ref[...], b_ref[...])
        o_ref[...] = y
        if len(rest) == 2:
            rest[0][...] = y.astype(BF16)


def _ffn_ln(x, w1, w2, g, b, emit_bf16, tm=512, tf=1024):
    m, d = x.shape
    f = w1.shape[1]
    full = lambda a: pl.BlockSpec(a.shape, lambda i, j: (0,) * a.ndim)
    out_shape = [jax.ShapeDtypeStruct((m, d), F32)]
    out_specs = [pl.BlockSpec((tm, d), lambda i, j: (i, 0))]
    if emit_bf16:
        out_shape.append(jax.ShapeDtypeStruct((m, d), BF16))
        out_specs.append(pl.BlockSpec((tm, d), lambda i, j: (i, 0)))
    return pl.pallas_call(
        _ffn_kernel,
        out_shape=out_shape,
        grid=(m // tm, f // tf),
        in_specs=[pl.BlockSpec((tm, d), lambda i, j: (i, 0)),
                  pl.BlockSpec((d, tf), lambda i, j: (0, j)),
                  pl.BlockSpec((tf, d), lambda i, j: (j, 0)),
                  full(g), full(b)],
        out_specs=out_specs,
        scratch_shapes=[pltpu.VMEM((tm, d), BF16)],
        compiler_params=_params(("parallel", "arbitrary")),
        name="ffn_ln",
    )(x, w1, w2, g, b)


def _rope(x, cos, sin_signed, is_first_half):
    partner = jnp.where(is_first_half, pltpu.roll(x, LANES - DH_D // 2, axis=1),
                        pltpu.roll(x, DH_D // 2, axis=1))
    return x * cos + partner * sin_signed


def _odd_in_kernel(x_ref, w_ref, cos_ref, sin_ref, xc_ref, qt_ref, k_ref, vt_ref, *, tq, tk):
    tm = x_ref.shape[0]
    j = pl.program_id(1)
    acc = _dot(x_ref[...], w_ref[...])
    lane = lax.broadcasted_iota(jnp.int32, (tm, LANES), 1)
    is_first_half = (lane % DH_D) < (DH_D // 2)

    @pl.when(j == 0)
    def _():
        xc_ref[...] = acc

    @pl.when(j == 1)
    def _():
        for h in range(H_D):
            cols = slice(h * LANES, (h + 1) * LANES)
            q = _rope(acc[:, cols], cos_ref[...], sin_ref[...], is_first_half) * Q_SCALE
            for c in range(tm // tq):
                qt_ref[c, cols, :] = q[c * tq:(c + 1) * tq, :].T.astype(BF16)

    @pl.when(j == 2)
    def _():
        for h in range(H_D):
            cols = slice(h * LANES, (h + 1) * LANES)
            k_ref[:, cols] = _rope(acc[:, cols], cos_ref[...], sin_ref[...], is_first_half).astype(BF16)

    @pl.when(j == 3)
    def _():
        pad_row = lax.broadcasted_iota(jnp.int32, (VT_ROWS - DV_D, tk), 0)
        pad = jnp.where(pad_row == 0, 1.0, 0.0).astype(BF16)
        for h in range(H_D):
            cols = slice(h * LANES, (h + 1) * LANES)
            for c in range(tm // tk):
                vt_ref[c, h * VT_ROWS:h * VT_ROWS + DV_D, :] = acc[c * tk:(c + 1) * tk, cols].T.astype(BF16)
                vt_ref[c, h * VT_ROWS + DV_D:(h + 1) * VT_ROWS, :] = pad


def _odd_in_proj(xb, w, cos, sin_signed, seq, tq, tk, tm=1024):
    m, d = xb.shape
    tn = D_C
    spt = seq // tm
    return pl.pallas_call(
        functools.partial(_odd_in_kernel, tq=tq, tk=tk),
        out_shape=(jax.ShapeDtypeStruct((m, D_C), F32),
                   jax.ShapeDtypeStruct((m // tq, D_D, tq), BF16),
                   jax.ShapeDtypeStruct((m, D_D), BF16),
                   jax.ShapeDtypeStruct((m // tk, H_D * VT_ROWS, tk), BF16)),
        grid=(m // tm, 4),
        in_specs=[pl.BlockSpec((tm, d), lambda i, j: (i, 0)),
                  pl.BlockSpec((d, tn), lambda i, j: (0, j)),
                  pl.BlockSpec((tm, LANES), lambda i, j: (i % spt, 0)),
                  pl.BlockSpec((tm, LANES), lambda i, j: (i % spt, 0))],
        out_specs=(pl.BlockSpec((tm, D_C), lambda i, j: (i, 0)),
                   pl.BlockSpec((tm // tq, D_D, tq), lambda i, j: (i, 0, 0)),
                   pl.BlockSpec((tm, D_D), lambda i, j: (i, 0)),
                   pl.BlockSpec((tm // tk, H_D * VT_ROWS, tk), lambda i, j: (i, 0, 0))),
        compiler_params=_params(("parallel", "arbitrary")),
        name="odd_in_proj",
    )(xb, w, cos, sin_signed)


def _fft_stage1_kernel(x_ref, f1h_ref, f1l_ref, wch_ref, wcl_ref, o_ref):
    n1 = x_ref.shape[1]
    groups = x_ref.shape[2] // LANES
    xh, xl = _split(x_ref[0])
    y = _dot3(f1h_ref[...], f1l_ref[...], xh, xl)
    yr = jnp.concatenate([y[:n1, g * LANES:(g + 1) * LANES] for g in range(groups)], axis=0)
    yi = jnp.concatenate([y[n1:, g * LANES:(g + 1) * LANES] for g in range(groups)], axis=0)
    ych, ycl = _split(jnp.concatenate([yr, yi], axis=1))
    u = _dot3(ych, ycl, wch_ref[...], wcl_ref[...])
    for g in range(groups):
        o_ref[0, 0, :, g * LANES:(g + 1) * LANES] = u[g * n1:(g + 1) * n1, :LANES]
        o_ref[0, 1, :, g * LANES:(g + 1) * LANES] = u[g * n1:(g + 1) * n1, LANES:]


def _fft_stage2_kernel(u_ref, f2c_ref, f2s_ref, tc_ref, ts_ref, o_ref, *, scale):
    kb = u_ref.shape[2]
    base = pl.program_id(1) * kb
    f2c = f2c_ref[...]
    f2s = f2s_ref[...]
    for kk in range(kb):
        tc = tc_ref[pl.ds(base + kk, 1), :]
        ts = ts_ref[pl.ds(base + kk, 1), :]
        gc = (f2c * tc - f2s * ts) * scale
        gs = (f2s * tc + f2c * ts) * scale
        gh, gl = _split(jnp.concatenate([gc, gs], axis=1))
        dh, dl = _split(jnp.concatenate([u_ref[0, 0, kk], u_ref[0, 1, kk]], axis=0))
        o_ref[0, :, kk * D_C:(kk + 1) * D_C] = _dot3(gh, gl, dh, dl).astype(BF16)


def _dft_tables(seq):
    n2 = CHUNK
    n1 = seq // n2
    two_pi = 2.0 * np.pi
    a1 = two_pi * np.outer(np.arange(n1), np.arange(n1)) / n1
    f1 = np.concatenate([np.cos(a1), -np.sin(a1)], axis=0)
    ac = two_pi * np.outer(np.arange(GC_W), np.arange(GC_W)) / GC_W
    wc = np.block([[np.cos(ac), -np.sin(ac)], [np.sin(ac), np.cos(ac)]])
    a2 = two_pi * np.outer(np.arange(n2), np.arange(n2)) / n2
    at = two_pi * np.outer(np.arange(n1), np.arange(n2)) / seq

    def split(a):
        a = jnp.asarray(a, F32)
        hi = a.astype(BF16)
        return hi, (a - hi.astype(F32)).astype(BF16)

    f32 = lambda a: jnp.asarray(a, F32)
    return split(f1) + split(wc) + (f32(np.cos(a2)), f32(np.sin(a2)), f32(np.cos(at)), f32(np.sin(at)))


def _fourier_mix(xc, batch, seq, jb=8, kb=4):
    n2 = CHUNK
    n1 = seq // n2
    f1h, f1l, wch, wcl, f2c, f2s, tc, ts = _dft_tables(seq)
    full2 = lambda a: pl.BlockSpec(a.shape, lambda b, i: (0,) * a.ndim)
    wblk = jb * D_C
    u = pl.pallas_call(
        _fft_stage1_kernel,
        out_shape=jax.ShapeDtypeStruct((batch, 2, n1, n2 * D_C), F32),
        grid=(batch, n2 // jb),
        in_specs=[pl.BlockSpec((1, n1, wblk), lambda b, i: (b, 0, i)),
                  full2(f1h), full2(f1l), full2(wch), full2(wcl)],
        out_specs=pl.BlockSpec((1, 2, n1, wblk), lambda b, i: (b, 0, 0, i)),
        compiler_params=_params(("parallel", "parallel")),
        name="fft_stage1",
    )(xc.reshape(batch, n1, n2 * D_C), f1h, f1l, wch, wcl)
    kb = min(kb, n1)
    yc = pl.pallas_call(
        functools.partial(_fft_stage2_kernel, scale=float(1.0 / math.sqrt(seq * GC_W))),
        out_shape=jax.ShapeDtypeStruct((batch, n2, n1 * D_C), BF16),
        grid=(batch, n1 // kb),
        in_specs=[pl.BlockSpec((1, 2, kb, n2, D_C), lambda b, i: (b, 0, i, 0, 0)),
                  full2(f2c), full2(f2s), full2(tc), full2(ts)],
        out_specs=pl.BlockSpec((1, n2, kb * D_C), lambda b, i: (b, 0, i)),
        compiler_params=_params(("parallel", "parallel")),
        name="fft_stage2",
    )(u.reshape(batch, 2, n1, n2, D_C), f2c, f2s, tc, ts)
    return yc.reshape(batch * seq, D_C)


def _attn_kernel(qt_ref, k_ref, vt_ref, lq1_ref, lk1_ref, lq2_ref, lk2_ref, g_ref, o_ref,
                 sa_ref, sb_ref, m_ref, acc_ref, *, lam_init):
    tq = qt_ref.shape[2]
    tk = vt_ref.shape[2]
    nk = vt_ref.shape[0]

    qt = qt_ref[0]
    row = lax.broadcasted_iota(jnp.int32, (LANES, tq), 0)
    zero = jnp.zeros_like(qt)
    q2 = jnp.concatenate([jnp.where(row < DH_D, qt, zero), jnp.where(row >= DH_D, qt, zero)], axis=1)

    m_ref[...] = jnp.full(m_ref.shape, -1e30, F32)
    acc_ref[...] = jnp.zeros(acc_ref.shape, F32)

    def scores(c):
        kc = k_ref[pl.ds(pl.multiple_of(c * tk, tk), tk), :]
        return _dot(kc, q2)

    def accumulate(s_ref, c):
        s = s_ref[...]
        m_old = m_ref[...]
        m_new = jnp.maximum(m_old, jnp.max(s, axis=0, keepdims=True))
        p = jnp.exp2(s - m_new).astype(BF16)
        acc_ref[...] = jnp.exp2(m_old - m_new) * acc_ref[...] + _dot(vt_ref[c], p)
        m_ref[...] = m_new

    bufs = (sa_ref, sb_ref)
    sa_ref[...] = scores(0)

    unroll = ATTN_UNROLL if nk >= 4 * ATTN_UNROLL else 2

    def run_chunks(c0, feed_next):
        for u in range(unroll):
            if u + 1 < unroll or feed_next:
                bufs[(u + 1) % 2][...] = scores(c0 + u + 1)
            accumulate(bufs[u % 2], c0 + u)

    def body(i, carry):
        run_chunks(unroll * i, True)
        return carry

    lax.fori_loop(0, nk // unroll - 1, body, 0)
    run_chunks(nk - unroll, False)

    lam = (jnp.exp(jnp.sum(lq1_ref[...] * lk1_ref[...], axis=1, keepdims=True))
           - jnp.exp(jnp.sum(lq2_ref[...] * lk2_ref[...], axis=1, keepdims=True)) + lam_init)
    on = acc_ref[:DV_D, :] / acc_ref[DV_D:DV_D + 1, :]
    d = on[:, :tq] - lam * on[:, tq:]
    ms = jnp.mean(d * d, axis=0, keepdims=True)
    y = d * lax.rsqrt(ms + LN_EPS) * g_ref[...] * (1.0 - lam_init)
    o_ref[...] = y.T.astype(BF16)


def _diff_attention(qt, k, vt, lq1, lk1, lq2, lk2, g, batch, seq, lam_init):
    tq = qt.shape[2]
    tk = vt.shape[2]
    m = batch * seq
    nq = seq // tq
    gb = jnp.broadcast_to(g.astype(F32).reshape(DV_D, 1), (DV_D, tq))
    vec = lambda a: a.astype(F32).reshape(1, DH_D)
    full = lambda a: pl.BlockSpec(a.shape, lambda b, h, i: (0,) * a.ndim)
    args = (qt, k, vt, vec(lq1), vec(lk1), vec(lq2), vec(lk2), gb)
    return pl.pallas_call(
        functools.partial(_attn_kernel, lam_init=lam_init),
        out_shape=jax.ShapeDtypeStruct((m, D_D), BF16),
        grid=(batch, H_D, nq),
        in_specs=[pl.BlockSpec((1, DV_D, tq), lambda b, h, i: (b * nq + i, h, 0)),
                  pl.BlockSpec((seq, DV_D), lambda b, h, i: (b, h)),
                  pl.BlockSpec((seq // tk, VT_ROWS, tk), lambda b, h, i: (b, h, 0)),
                  full(args[3]), full(args[4]), full(args[5]), full(args[6]), full(gb)],
        out_specs=pl.BlockSpec((tq, DV_D), lambda b, h, i: (b * nq + i, h)),
        scratch_shapes=[pltpu.VMEM((tk, 2 * tq), F32), pltpu.VMEM((tk, 2 * tq), F32),
                        pltpu.VMEM((1, 2 * tq), F32), pltpu.VMEM((VT_ROWS, 2 * tq), F32)],
        compiler_params=_params(("parallel", "parallel", "arbitrary")),
        name="diff_attention",
    )(*args)


def _rope_tables(seq):
    half = DH_D // 2
    inv_freq = 1.0 / (ROPE_THETA ** (jnp.arange(half, dtype=F32) / half))
    ang = jnp.arange(seq, dtype=F32)[:, None] * inv_freq[None, :]
    cos = jnp.tile(jnp.cos(ang), (1, LANES // half))
    sign = jnp.tile(jnp.concatenate([-jnp.ones((half,), F32), jnp.ones((half,), F32)]), LANES // DH_D)
    sin_signed = jnp.tile(jnp.sin(ang), (1, LANES // half)) * sign[None, :]
    return cos, sin_signed


def _trunk(x3, p):
    batch, seq, d = x3.shape
    x = x3.reshape(batch * seq, d)
    row = lambda a: a.astype(F32).reshape(1, -1)
    for l in range(DEPTH):
        i = l // 2
        if l % 2 == 0:
            hcat = _even_in_proj(x, p["w_in_ab"][i])
            bs = jnp.broadcast_to(p["sgu_b"][i].astype(F32)[:, :, None], (H_A, CHUNK, LANES))
            ya, yb = _even_mix(hcat, seq, row(p["sgu_ln_g"][i]), row(p["sgu_ln_b"][i]),
                               p["sgu_w"][i], bs, p["conv_w"][i].astype(F32))
            w_out = p["w_out_ab"][i]
            x = _out_proj_ln(ya, yb, x, w_out[:D_A], w_out[D_A:], row(p["ln1_g"][l]), row(p["ln1_b"][l]))
        else:
            lam_init = 0.8 - 0.6 * math.exp(-0.3 * l)
            cos, sin_signed = _rope_tables(seq)
            xc, qt, k, vt = _odd_in_proj(xb, p["w_in_cd"][i], cos, sin_signed, seq, tq=256, tk=512)
            yc = _fourier_mix(xc, batch, seq)
            yd = _diff_attention(qt, k, vt, p["lambda_q1"][i], p["lambda_k1"][i], p["lambda_q2"][i],
                                 p["lambda_k2"][i], p["subln_g"][i], batch, seq, lam_init)
            w_out = p["w_out_cd"][i]
            x = _out_proj_ln(yc, yd, x, w_out[:D_C], w_out[D_C:], row(p["ln1_g"][l]), row(p["ln1_b"][l]))
        last = l == DEPTH - 1
        outs = _ffn_ln(x, p["w_ff1"][l], p["w_ff2"][l], row(p["ln2_g"][l]), row(p["ln2_b"][l]),
                       emit_bf16=not last)
        x = outs[0]
        xb = None if last else outs[1]
    return x.reshape(batch, seq, d)


def kernel(x_prompt, x_sample, w_in_ab, sgu_ln_g, sgu_ln_b, sgu_w, sgu_b, conv_w, w_out_ab, w_in_cd,
           lambda_q1, lambda_k1, lambda_q2, lambda_k2, subln_g, w_out_cd, ln1_g, ln1_b, ln2_g, ln2_b,
           w_ff1, w_ff2):
    p = dict(w_in_ab=w_in_ab.astype(BF16), sgu_ln_g=sgu_ln_g, sgu_ln_b=sgu_ln_b, sgu_w=sgu_w.astype(BF16),
             sgu_b=sgu_b, conv_w=conv_w, w_out_ab=w_out_ab.astype(BF16), w_in_cd=w_in_cd.astype(BF16),
             lambda_q1=lambda_q1, lambda_k1=lambda_k1, lambda_q2=lambda_q2, lambda_k2=lambda_k2,
             subln_g=subln_g, w_out_cd=w_out_cd.astype(BF16), ln1_g=ln1_g, ln1_b=ln1_b, ln2_g=ln2_g,
             ln2_b=ln2_b, w_ff1=w_ff1.astype(BF16), w_ff2=w_ff2.astype(BF16))
    return (_trunk(x_prompt, p), _trunk(x_sample, p))
```

```python
import functools
import math

import numpy as np
import jax
import jax.numpy as jnp
from jax import lax
from jax.experimental import pallas as pl
from jax.experimental.pallas import tpu as pltpu

D_MODEL = 2048
DEPTH = 2
CHUNK = 128
D_A = D_MODEL // 2
H_A = 8
D_B = D_MODEL // 2
D_C = D_MODEL // 2
GC_W = 128
H_D = 8
DH_D = 64
DV_D = 2 * DH_D
D_D = H_D * DV_D
D_FF = 4 * D_MODEL
ROPE_THETA = 10000.0
LN_EPS = 1e-5
ALPHA = (2 * DEPTH) ** 0.25
Q_SCALE = DH_D ** -0.5 * math.log2(math.e)
VT_ROWS = DV_D + 16
ATTN_UNROLL = 4

LANES = 128
VMEM_LIMIT_BYTES = 56 * 1024 * 1024

BF16 = jnp.bfloat16
F32 = jnp.float32


def _params(semantics):
    return pltpu.CompilerParams(dimension_semantics=semantics,
                                vmem_limit_bytes=VMEM_LIMIT_BYTES)


def _dot(a, b):
    return jnp.dot(a, b, preferred_element_type=F32)


def _split(x):
    hi = x.astype(BF16)
    lo = (x - hi.astype(F32)).astype(BF16)
    return hi, lo


def _dot3(a_hi, a_lo, b_hi, b_lo):
    return _dot(a_hi, b_hi) + (_dot(a_lo, b_hi) + _dot(a_hi, b_lo))


def _layer_norm(x, g, b):
    mu = jnp.mean(x, axis=-1, keepdims=True)
    xc = x - mu
    var = jnp.mean(xc * xc, axis=-1, keepdims=True)
    return xc * lax.rsqrt(var + LN_EPS) * g + b


def _even_in_kernel(x_ref, w_ref, o_ref, xb_ref):
    j = pl.program_id(1)

    @pl.when(j == 0)
    def _():
        xb_ref[...] = x_ref[...].astype(BF16)

    acc = _dot(xb_ref[...], w_ref[...])

    @pl.when(j < 2)
    def _():
        o_ref[...] = jax.nn.gelu(acc)

    @pl.when(j >= 2)
    def _():
        o_ref[...] = acc


def _even_in_proj(x, w, tm=1024, tn=1024):
    m, d = x.shape
    n = w.shape[1]
    return pl.pallas_call(
        _even_in_kernel,
        out_shape=jax.ShapeDtypeStruct((m, n), F32),
        grid=(m // tm, n // tn),
        in_specs=[pl.BlockSpec((tm, d), lambda i, j: (i, 0)),
                  pl.BlockSpec((d, tn), lambda i, j: (0, j))],
        out_specs=pl.BlockSpec((tm, tn), lambda i, j: (i, j)),
        scratch_shapes=[pltpu.VMEM((tm, d), BF16)],
        compiler_params=_params(("parallel", "arbitrary")),
        name="even_in_proj",
    )(x, w)


def _even_mix_kernel(u_ref, v_ref, bg_ref, cg_ref, xp_ref, cgp_ref, xpp_ref, cgn_ref, xpn_ref,
                     lng_ref, lnb_ref, ws_ref, bs_ref, cw_ref, ya_ref, yb_ref, *, tiles_per_seq):
    tm = u_ref.shape[0]
    i = pl.program_id(0)
    t = i % tiles_per_seq
    not_first = (t != 0).astype(F32)
    not_last = (t != tiles_per_seq - 1).astype(F32)

    z = cg_ref[...] * xp_ref[...]
    z_before = cgp_ref[7:8, :] * xpp_ref[7:8, :] * not_first
    z_after = cgn_ref[0:1, :] * xpn_ref[0:1, :] * not_last
    row = lax.broadcasted_iota(jnp.int32, (tm, 1), 0)
    z_dn = jnp.where(row == 0, z_before, pltpu.roll(z, 1, axis=0))
    z_up = jnp.where(row == tm - 1, z_after, pltpu.roll(z, tm - 1, axis=0))
    conv = cw_ref[0:1, :] * z_dn + cw_ref[1:2, :] * z + cw_ref[2:3, :] * z_up
    yb_ref[...] = (bg_ref[...] * conv).astype(BF16)

    vn = _layer_norm(v_ref[...], lng_ref[...], lnb_ref[...]).astype(BF16)
    for c in range(tm // CHUNK):
        rows = slice(c * CHUNK, (c + 1) * CHUNK)
        for h in range(H_A):
            cols = slice(h * LANES, (h + 1) * LANES)
            mixed = _dot(ws_ref[h], vn[rows, cols]) + bs_ref[h]
            ya_ref[rows, cols] = (u_ref[rows, cols] * mixed).astype(BF16)


def _even_mix(hcat, seq, ln_g, ln_b, ws, bs, cw, tm=512):
    m = hcat.shape[0]
    w = D_A
    nb8 = m // 8
    r8 = tm // 8
    col = lambda c: pl.BlockSpec((tm, w), lambda i, c=c: (i, c))
    prev = lambda c: pl.BlockSpec((8, w), lambda i, c=c: (jnp.maximum(i * r8 - 1, 0), c))
    nxt = lambda c: pl.BlockSpec((8, w), lambda i, c=c: (jnp.minimum((i + 1) * r8, nb8 - 1), c))
    full = lambda a: pl.BlockSpec(a.shape, lambda i: (0,) * a.ndim)
    return pl.pallas_call(
        functools.partial(_even_mix_kernel, tiles_per_seq=seq // tm),
        out_shape=(jax.ShapeDtypeStruct((m, w), BF16), jax.ShapeDtypeStruct((m, w), BF16)),
        grid=(m // tm,),
        in_specs=[col(0), col(1), col(2), col(3), col(4), prev(3), prev(4), nxt(3), nxt(4),
                  full(ln_g), full(ln_b), full(ws), full(bs), full(cw)],
        out_specs=(pl.BlockSpec((tm, w), lambda i: (i, 0)), pl.BlockSpec((tm, w), lambda i: (i, 0))),
        compiler_params=_params(("parallel",)),
        name="even_mix",
    )(hcat, hcat, hcat, hcat, hcat, hcat, hcat, hcat, hcat, ln_g, ln_b, ws, bs, cw)


def _out_proj_kernel(a1_ref, a2_ref, x_ref, w1_ref, w2_ref, g_ref, b_ref, o_ref):
    mix = _dot(a1_ref[...], w1_ref[...]) + _dot(a2_ref[...], w2_ref[...])
    o_ref[...] = _layer_norm(ALPHA * x_ref[...] + mix, g_ref[...], b_ref[...])


def _out_proj_ln(a1, a2, x, w1, w2, g, b, tm=512):
    m, d = x.shape
    k1, k2 = a1.shape[1], a2.shape[1]
    full = lambda a: pl.BlockSpec(a.shape, lambda i: (0,) * a.ndim)
    return pl.pallas_call(
        _out_proj_kernel,
        out_shape=jax.ShapeDtypeStruct((m, d), F32),
        grid=(m // tm,),
        in_specs=[pl.BlockSpec((tm, k1), lambda i: (i, 0)), pl.BlockSpec((tm, k2), lambda i: (i, 0)),
                  pl.BlockSpec((tm, d), lambda i: (i, 0)), full(w1), full(w2), full(g), full(b)],
        out_specs=pl.BlockSpec((tm, d), lambda i: (i, 0)),
        compiler_params=_params(("parallel",)),
        name="out_proj_ln",
    )(a1, a2, x, w1, w2, g, b)


def _ffn_kernel(x_ref, w1_ref, w2_ref, g_ref, b_ref, o_ref, *rest):
    xb_ref = rest[-1]
    j = pl.program_id(1)

    @pl.when(j == 0)
    def _():
        x = x_ref[...]
        xb_ref[...] = x.astype(BF16)
        o_ref[...] = ALPHA * x

    h = jnp.maximum(_dot(xb_ref[...], w1_ref[...]), 0.0)
    o_ref[...] += _dot((h * h).astype(BF16), w2_ref[...])

    @pl.when(j == pl.num_programs(1) - 1)
    def _():
        y = _layer_norm(o_ref[...], g_ref[...], b_ref[...])
        o_ref[...] = y
        if len(rest) == 2:
            rest[0][...] = y.astype(BF16)


def _ffn_ln(x, w1, w2, g, b, emit_bf16, tm=512, tf=1024):
    m, d = x.shape
    f = w1.shape[1]
    full = lambda a: pl.BlockSpec(a.shape, lambda i, j: (0,) * a.ndim)
    out_shape = [jax.ShapeDtypeStruct((m, d), F32)]
    out_specs = [pl.BlockSpec((tm, d), lambda i, j: (i, 0))]
    if emit_bf16:
        out_shape.append(jax.ShapeDtypeStruct((m, d), BF16))
        out_specs.append(pl.BlockSpec((tm, d), lambda i, j: (i, 0)))
    return pl.pallas_call(
        _ffn_kernel,
        out_shape=out_shape,
        grid=(m // tm, f // tf),
        in_specs=[pl.BlockSpec((tm, d), lambda i, j: (i, 0)),
                  pl.BlockSpec((d, tf), lambda i, j: (0, j)),
                  pl.BlockSpec((tf, d), lambda i, j: (j, 0)),
                  full(g), full(b)],
        out_specs=out_specs,
        scratch_shapes=[pltpu.VMEM((tm, d), BF16)],
        compiler_params=_params(("parallel", "arbitrary")),
        name="ffn_ln",
    )(x, w1, w2, g, b)


def _rope(x, cos, sin_signed, is_first_half):
    partner = jnp.where(is_first_half, pltpu.roll(x, LANES - DH_D // 2, axis=1),
                        pltpu.roll(x, DH_D // 2, axis=1))
    return x * cos + partner * sin_signed


def _odd_in_kernel(x_ref, w_ref, cos_ref, sin_ref, xc_ref, qt_ref, k_ref, vt_ref, *, tq, tk):
    tm = x_ref.shape[0]
    j = pl.program_id(1)
    acc = _dot(x_ref[...], w_ref[...])
    lane = lax.broadcasted_iota(jnp.int32, (tm, LANES), 1)
    is_first_half = (lane % DH_D) < (DH_D // 2)

    @pl.when(j == 0)
    def _():
        xc_ref[...] = acc

    @pl.when(j == 1)
    def _():
        for h in range(H_D):
            cols = slice(h * LANES, (h + 1) * LANES)
            q = _rope(acc[:, cols], cos_ref[...], sin_ref[...], is_first_half) * Q_SCALE
            for c in range(tm // tq):
                qt_ref[c, cols, :] = q[c * tq:(c + 1) * tq, :].T.astype(BF16)

    @pl.when(j == 2)
    def _():
        for h in range(H_D):
            cols = slice(h * LANES, (h + 1) * LANES)
            k_ref[:, cols] = _rope(acc[:, cols], cos_ref[...], sin_ref[...], is_first_half).astype(BF16)

    @pl.when(j == 3)
    def _():
        pad_row = lax.broadcasted_iota(jnp.int32, (VT_ROWS - DV_D, tk), 0)
        pad = jnp.where(pad_row == 0, 1.0, 0.0).astype(BF16)
        for h in range(H_D):
            cols = slice(h * LANES, (h + 1) * LANES)
            for c in range(tm // tk):
                vt_ref[c, h * VT_ROWS:h * VT_ROWS + DV_D, :] = acc[c * tk:(c + 1) * tk, cols].T.astype(BF16)
                vt_ref[c, h * VT_ROWS + DV_D:(h + 1) * VT_ROWS, :] = pad


def _odd_in_proj(xb, w, cos, sin_signed, seq, tq, tk, tm=1024):
    m, d = xb.shape
    tn = D_C
    spt = seq // tm
    return pl.pallas_call(
        functools.partial(_odd_in_kernel, tq=tq, tk=tk),
        out_shape=(jax.ShapeDtypeStruct((m, D_C), F32),
                   jax.ShapeDtypeStruct((m // tq, D_D, tq), BF16),
                   jax.ShapeDtypeStruct((m, D_D), BF16),
                   jax.ShapeDtypeStruct((m // tk, H_D * VT_ROWS, tk), BF16)),
        grid=(m // tm, 4),
        in_specs=[pl.BlockSpec((tm, d), lambda i, j: (i, 0)),
                  pl.BlockSpec((d, tn), lambda i, j: (0, j)),
                  pl.BlockSpec((tm, LANES), lambda i, j: (i % spt, 0)),
                  pl.BlockSpec((tm, LANES), lambda i, j: (i % spt, 0))],
        out_specs=(pl.BlockSpec((tm, D_C), lambda i, j: (i, 0)),
                   pl.BlockSpec((tm // tq, D_D, tq), lambda i, j: (i, 0, 0)),
                   pl.BlockSpec((tm, D_D), lambda i, j: (i, 0)),
                   pl.BlockSpec((tm // tk, H_D * VT_ROWS, tk), lambda i, j: (i, 0, 0))),
        compiler_params=_params(("parallel", "arbitrary")),
        name="odd_in_proj",
    )(xb, w, cos, sin_signed)


def _fft_stage1_kernel(x_ref, f1h_ref, f1l_ref, wch_ref, wcl_ref, o_ref):
    n1 = x_ref.shape[1]
    groups = x_ref.shape[2] // LANES
    xh, xl = _split(x_ref[0])
    y = _dot3(f1h_ref[...], f1l_ref[...], xh, xl)
    yr = jnp.concatenate([y[:n1, g * LANES:(g + 1) * LANES] for g in range(groups)], axis=0)
    yi = jnp.concatenate([y[n1:, g * LANES:(g + 1) * LANES] for g in range(groups)], axis=0)
    ych, ycl = _split(jnp.concatenate([yr, yi], axis=1))
    u = _dot3(ych, ycl, wch_ref[...], wcl_ref[...])
    for g in range(groups):
        o_ref[0, 0, :, g * LANES:(g + 1) * LANES] = u[g * n1:(g + 1) * n1, :LANES]
        o_ref[0, 1, :, g * LANES:(g + 1) * LANES] = u[g * n1:(g + 1) * n1, LANES:]


def _fft_stage2_kernel(u_ref, f2c_ref, f2s_ref, tc_ref, ts_ref, o_ref, *, scale):
    kb = u_ref.shape[2]
    base = pl.program_id(1) * kb
    f2c = f2c_ref[...]
    f2s = f2s_ref[...]
    for kk in range(kb):
        tc = tc_ref[pl.ds(base + kk, 1), :]
        ts = ts_ref[pl.ds(base + kk, 1), :]
        gc = (f2c * tc - f2s * ts) * scale
        gs = (f2s * tc + f2c * ts) * scale
        gh, gl = _split(jnp.concatenate([gc, gs], axis=1))
        dh, dl = _split(jnp.concatenate([u_ref[0, 0, kk], u_ref[0, 1, kk]], axis=0))
        o_ref[0, :, kk * D_C:(kk + 1) * D_C] = _dot3(gh, gl, dh, dl).astype(BF16)


def _dft_tables(seq):
    n2 = CHUNK
    n1 = seq // n2
    two_pi = 2.0 * np.pi
    a1 = two_pi * np.outer(np.arange(n1), np.arange(n1)) / n1
    f1 = np.concatenate([np.cos(a1), -np.sin(a1)], axis=0)
    ac = two_pi * np.outer(np.arange(GC_W), np.arange(GC_W)) / GC_W
    wc = np.block([[np.cos(ac), -np.sin(ac)], [np.sin(ac), np.cos(ac)]])
    a2 = two_pi * np.outer(np.arange(n2), np.arange(n2)) / n2
    at = two_pi * np.outer(np.arange(n1), np.arange(n2)) / seq

    def split(a):
        a = jnp.asarray(a, F32)
        hi = a.astype(BF16)
        return hi, (a - hi.astype(F32)).astype(BF16)

    f32 = lambda a: jnp.asarray(a, F32)
    return split(f1) + split(wc) + (f32(np.cos(a2)), f32(np.sin(a2)), f32(np.cos(at)), f32(np.sin(at)))


def _fourier_mix(xc, batch, seq, jb=8, kb=4):
    n2 = CHUNK
    n1 = seq // n2
    f1h, f1l, wch, wcl, f2c, f2s, tc, ts = _dft_tables(seq)
    full2 = lambda a: pl.BlockSpec(a.shape, lambda b, i: (0,) * a.ndim)
    wblk = jb * D_C
    u = pl.pallas_call(
        _fft_stage1_kernel,
        out_shape=jax.ShapeDtypeStruct((batch, 2, n1, n2 * D_C), F32),
        grid=(batch, n2 // jb),
        in_specs=[pl.BlockSpec((1, n1, wblk), lambda b, i: (b, 0, i)),
                  full2(f1h), full2(f1l), full2(wch), full2(wcl)],
        out_specs=pl.BlockSpec((1, 2, n1, wblk), lambda b, i: (b, 0, 0, i)),
        compiler_params=_params(("parallel", "parallel")),
        name="fft_stage1",
    )(xc.reshape(batch, n1, n2 * D_C), f1h, f1l, wch, wcl)
    kb = min(kb, n1)
    yc = pl.pallas_call(
        functools.partial(_fft_stage2_kernel, scale=float(1.0 / math.sqrt(seq * GC_W))),
        out_shape=jax.ShapeDtypeStruct((batch, n2, n1 * D_C), BF16),
        grid=(batch, n1 // kb),
        in_specs=[pl.BlockSpec((1, 2, kb, n2, D_C), lambda b, i: (b, 0, i, 0, 0)),
                  full2(f2c), full2(f2s), full2(tc), full2(ts)],
        out_specs=pl.BlockSpec((1, n2, kb * D_C), lambda b, i: (b, 0, i)),
        compiler_params=_params(("parallel", "parallel")),
        name="fft_stage2",
    )(u.reshape(batch, 2, n1, n2, D_C), f2c, f2s, tc, ts)
    return yc.reshape(batch * seq, D_C)


def _attn_kernel(qt_ref, k_ref, vt_ref, lq1_ref, lk1_ref, lq2_ref, lk2_ref, g_ref, o_ref,
                 sa_ref, sb_ref, m_ref, acc_ref, *, lam_init):
    tq = qt_ref.shape[2]
    tk = vt_ref.shape[2]
    nk = vt_ref.shape[0]

    qt = qt_ref[0]
    row = lax.broadcasted_iota(jnp.int32, (LANES, tq), 0)
    zero = jnp.zeros_like(qt)
    q2 = jnp.concatenate([jnp.where(row < DH_D, qt, zero), jnp.where(row >= DH_D, qt, zero)], axis=1)

    m_ref[...] = jnp.full(m_ref.shape, -1e30, F32)
    acc_ref[...] = jnp.zeros(acc_ref.shape, F32)

    def scores(c):
        kc = k_ref[pl.ds(pl.multiple_of(c * tk, tk), tk), :]
        return _dot(kc, q2)

    def accumulate(s_ref, c):
        s = s_ref[...]
        m_old = m_ref[...]
        m_new = jnp.maximum(m_old, jnp.max(s, axis=0, keepdims=True))
        p = jnp.exp2(s - m_new).astype(BF16)
        acc_ref[...] = jnp.exp2(m_old - m_new) * acc_ref[...] + _dot(vt_ref[c], p)
        m_ref[...] = m_new

    bufs = (sa_ref, sb_ref)
    sa_ref[...] = scores(0)

    unroll = ATTN_UNROLL if nk >= 4 * ATTN_UNROLL else 2

    def run_chunks(c0, feed_next):
        for u in range(unroll):
            if u + 1 < unroll or feed_next:
                bufs[(u + 1) % 2][...] = scores(c0 + u + 1)
            accumulate(bufs[u % 2], c0 + u)

    def body(i, carry):
        run_chunks(unroll * i, True)
        return carry

    lax.fori_loop(0, nk // unroll - 1, body, 0)
    run_chunks(nk - unroll, False)

    lam = (jnp.exp(jnp.sum(lq1_ref[...] * lk1_ref[...], axis=1, keepdims=True))
           - jnp.exp(jnp.sum(lq2_ref[...] * lk2_ref[...], axis=1, keepdims=True)) + lam_init)
    on = acc_ref[:DV_D, :] / acc_ref[DV_D:DV_D + 1, :]
    d = on[:, :tq] - lam * on[:, tq:]
    ms = jnp.mean(d * d, axis=0, keepdims=True)
    y = d * lax.rsqrt(ms + LN_EPS) * g_ref[...] * (1.0 - lam_init)
    o_ref[...] = y.T.astype(BF16)


def _diff_attention(qt, k, vt, lq1, lk1, lq2, lk2, g, batch, seq, lam_init):
    tq = qt.shape[2]
    tk = vt.shape[2]
    m = batch * seq
    nq = seq // tq
    gb = jnp.broadcast_to(g.astype(F32).reshape(DV_D, 1), (DV_D, tq))
    vec = lambda a: a.astype(F32).reshape(1, DH_D)
    full = lambda a: pl.BlockSpec(a.shape, lambda b, h, i: (0,) * a.ndim)
    args = (qt, k, vt, vec(lq1), vec(lk1), vec(lq2), vec(lk2), gb)
    return pl.pallas_call(
        functools.partial(_attn_kernel, lam_init=lam_init),
        out_shape=jax.ShapeDtypeStruct((m, D_D), BF16),
        grid=(batch, H_D, nq),
        in_specs=[pl.BlockSpec((1, DV_D, tq), lambda b, h, i: (b * nq + i, h, 0)),
                  pl.BlockSpec((seq, DV_D), lambda b, h, i: (b, h)),
                  pl.BlockSpec((seq // tk, VT_ROWS, tk), lambda b, h, i: (b, h, 0)),
                  full(args[3]), full(args[4]), full(args[5]), full(args[6]), full(gb)],
        out_specs=pl.BlockSpec((tq, DV_D), lambda b, h, i: (b * nq + i, h)),
        scratch_shapes=[pltpu.VMEM((tk, 2 * tq), F32), pltpu.VMEM((tk, 2 * tq), F32),
                        pltpu.VMEM((1, 2 * tq), F32), pltpu.VMEM((VT_ROWS, 2 * tq), F32)],
        compiler_params=_params(("parallel", "parallel", "arbitrary")),
        name="diff_attention",
    )(*args)


def _rope_tables(seq):
    half = DH_D // 2
    inv_freq = 1.0 / (ROPE_THETA ** (jnp.arange(half, dtype=F32) / half))
    ang = jnp.arange(seq, dtype=F32)[:, None] * inv_freq[None, :]
    cos = jnp.tile(jnp.cos(ang), (1, LANES // half))
    sign = jnp.tile(jnp.concatenate([-jnp.ones((half,), F32), jnp.ones((half,), F32)]), LANES // DH_D)
    sin_signed = jnp.tile(jnp.sin(ang), (1, LANES // half)) * sign[None, :]
    return cos, sin_signed


def _trunk(x3, p):
    batch, seq, d = x3.shape
    x = x3.reshape(batch * seq, d)
    row = lambda a: a.astype(F32).reshape(1, -1)
    for l in range(DEPTH):
        i = l // 2
        if l % 2 == 0:
            hcat = _even_in_proj(x, p["w_in_ab"][i])
            bs = jnp.broadcast_to(p["sgu_b"][i].astype(F32)[:, :, None], (H_A, CHUNK, LANES))
            ya, yb = _even_mix(hcat, seq, row(p["sgu_ln_g"][i]), row(p["sgu_ln_b"][i]),
                               p["sgu_w"][i], bs, p["conv_w"][i].astype(F32))
            w_out = p["w_out_ab"][i]
            x = _out_proj_ln(ya, yb, x, w_out[:D_A], w_out[D_A:], row(p["ln1_g"][l]), row(p["ln1_b"][l]))
        else:
            lam_init = 0.8 - 0.6 * math.exp(-0.3 * l)
            cos, sin_signed = _rope_tables(seq)
            xc, qt, k, vt = _odd_in_proj(xb, p["w_in_cd"][i], cos, sin_signed, seq, tq=512, tk=512)
            yc = _fourier_mix(xc, batch, seq)
            yd = _diff_attention(qt, k, vt, p["lambda_q1"][i], p["lambda_k1"][i], p["lambda_q2"][i],
                                 p["lambda_k2"][i], p["subln_g"][i], batch, seq, lam_init)
            w_out = p["w_out_cd"][i]
            x = _out_proj_ln(yc, yd, x, w_out[:D_C], w_out[D_C:], row(p["ln1_g"][l]), row(p["ln1_b"][l]))
        last = l == DEPTH - 1
        outs = _ffn_ln(x, p["w_ff1"][l], p["w_ff2"][l], row(p["ln2_g"][l]), row(p["ln2_b"][l]),
                       emit_bf16=not last)
        x = outs[0]
        xb = None if last else outs[1]
    return x.reshape(batch, seq, d)


def kernel(x_prompt, x_sample, w_in_ab, sgu_ln_g, sgu_ln_b, sgu_w, sgu_b, conv_w, w_out_ab, w_in_cd,
           lambda_q1, lambda_k1, lambda_q2, lambda_k2, subln_g, w_out_cd, ln1_g, ln1_b, ln2_g, ln2_b,
           w_ff1, w_ff2):
    p = dict(w_in_ab=w_in_ab.astype(BF16), sgu_ln_g=sgu_ln_g, sgu_ln_b=sgu_ln_b, sgu_w=sgu_w.astype(BF16),
             sgu_b=sgu_b, conv_w=conv_w, w_out_ab=w_out_ab.astype(BF16), w_in_cd=w_in_cd.astype(BF16),
             lambda_q1=lambda_q1, lambda_k1=lambda_k1, lambda_q2=lambda_q2, lambda_k2=lambda_k2,
             subln_g=subln_g, w_out_cd=w_out_cd.astype(BF16), ln1_g=ln1_g, ln1_b=ln1_b, ln2_g=ln2_g,
             ln2_b=ln2_b, w_ff1=w_ff1.astype(BF16), w_ff2=w_ff2.astype(BF16))
    return (_trunk(x_prompt, p), _trunk(x_sample, p))
```

```python
import functools
import math

import numpy as np
import jax
import jax.numpy as jnp
from jax import lax
from jax.experimental import pallas as pl
from jax.experimental.pallas import tpu as pltpu

D_MODEL = 2048
DEPTH = 2
CHUNK = 128
D_A = D_MODEL // 2
H_A = 8
D_B = D_MODEL // 2
D_C = D_MODEL // 2
GC_W = 128
H_D = 8
DH_D = 64
DV_D = 2 * DH_D
D_D = H_D * DV_D
D_FF = 4 * D_MODEL
ROPE_THETA = 10000.0
LN_EPS = 1e-5
ALPHA = (2 * DEPTH) ** 0.25
Q_SCALE = DH_D ** -0.5 * math.log2(math.e)
VT_ROWS = DV_D + 16
ATTN_UNROLL = 4

LANES = 128
HALO_ROWS = 16
VMEM_LIMIT_BYTES = 56 * 1024 * 1024

BF16 = jnp.bfloat16
F32 = jnp.float32


def _params(semantics):
    return pltpu.CompilerParams(dimension_semantics=semantics,
                                vmem_limit_bytes=VMEM_LIMIT_BYTES)


def _dot(a, b):
    return jnp.dot(a, b, preferred_element_type=F32)


def _split(x):
    hi = x.astype(BF16)
    lo = (x - hi.astype(F32)).astype(BF16)
    return hi, lo


def _dot3(a_hi, a_lo, b_hi, b_lo):
    return _dot(a_hi, b_hi) + (_dot(a_lo, b_hi) + _dot(a_hi, b_lo))


def _layer_norm(x, g, b):
    mu = jnp.mean(x, axis=-1, keepdims=True)
    xc = x - mu
    var = jnp.mean(xc * xc, axis=-1, keepdims=True)
    return xc * lax.rsqrt(var + LN_EPS) * g + b


def _even_in_kernel(x_ref, w_ref, o_ref, xb_ref):
    j = pl.program_id(1)

    @pl.when(j == 0)
    def _():
        xb_ref[...] = x_ref[...].astype(BF16)

    acc = _dot(xb_ref[...], w_ref[...])

    @pl.when(j < 2)
    def _():
        o_ref[...] = jax.nn.gelu(acc).astype(o_ref.dtype)

    @pl.when(j >= 2)
    def _():
        o_ref[...] = acc.astype(o_ref.dtype)


def _even_in_proj(x, w, tm=1024, tn=1024):
    m, d = x.shape
    n = w.shape[1]
    return pl.pallas_call(
        _even_in_kernel,
        out_shape=jax.ShapeDtypeStruct((m, n), BF16),
        grid=(m // tm, n // tn),
        in_specs=[pl.BlockSpec((tm, d), lambda i, j: (i, 0)),
                  pl.BlockSpec((d, tn), lambda i, j: (0, j))],
        out_specs=pl.BlockSpec((tm, tn), lambda i, j: (i, j)),
        scratch_shapes=[pltpu.VMEM((tm, d), BF16)],
        compiler_params=_params(("parallel", "arbitrary")),
        name="even_in_proj",
    )(x, w)


def _even_mix_kernel(u_ref, v_ref, bg_ref, cg_ref, xp_ref, cgp_ref, xpp_ref, cgn_ref, xpn_ref,
                     lng_ref, lnb_ref, ws_ref, bs_ref, cw_ref, ya_ref, yb_ref, *, tiles_per_seq):
    tm = u_ref.shape[0]
    i = pl.program_id(0)
    t = i % tiles_per_seq
    not_first = (t != 0).astype(F32)
    not_last = (t != tiles_per_seq - 1).astype(F32)

    f32 = lambda ref: ref[...].astype(F32)
    z = f32(cg_ref) * f32(xp_ref)
    z_before = (f32(cgp_ref) * f32(xpp_ref))[HALO_ROWS - 1:, :] * not_first
    z_after = (f32(cgn_ref) * f32(xpn_ref))[0:1, :] * not_last
    row = lax.broadcasted_iota(jnp.int32, (tm, 1), 0)
    z_dn = jnp.where(row == 0, z_before, pltpu.roll(z, 1, axis=0))
    z_up = jnp.where(row == tm - 1, z_after, pltpu.roll(z, tm - 1, axis=0))
    conv = cw_ref[0:1, :] * z_dn + cw_ref[1:2, :] * z + cw_ref[2:3, :] * z_up
    yb_ref[...] = (f32(bg_ref) * conv).astype(BF16)

    vn = _layer_norm(f32(v_ref), lng_ref[...], lnb_ref[...]).astype(BF16)
    for c in range(tm // CHUNK):
        rows = slice(c * CHUNK, (c + 1) * CHUNK)
        for h in range(H_A):
            cols = slice(h * LANES, (h + 1) * LANES)
            mixed = _dot(ws_ref[h], vn[rows, cols]) + bs_ref[h]
            ya_ref[rows, cols] = (u_ref[rows, cols].astype(F32) * mixed).astype(BF16)


def _even_mix(hcat, seq, ln_g, ln_b, ws, bs, cw, tm=512):
    m = hcat.shape[0]
    w = D_A
    nb8 = m // HALO_ROWS
    r8 = tm // HALO_ROWS
    col = lambda c: pl.BlockSpec((tm, w), lambda i, c=c: (i, c))
    prev = lambda c: pl.BlockSpec((HALO_ROWS, w), lambda i, c=c: (jnp.maximum(i * r8 - 1, 0), c))
    nxt = lambda c: pl.BlockSpec((HALO_ROWS, w), lambda i, c=c: (jnp.minimum((i + 1) * r8, nb8 - 1), c))
    full = lambda a: pl.BlockSpec(a.shape, lambda i: (0,) * a.ndim)
    return pl.pallas_call(
        functools.partial(_even_mix_kernel, tiles_per_seq=seq // tm),
        out_shape=(jax.ShapeDtypeStruct((m, w), BF16), jax.ShapeDtypeStruct((m, w), BF16)),
        grid=(m // tm,),
        in_specs=[col(0), col(1), col(2), col(3), col(4), prev(3), prev(4), nxt(3), nxt(4),
                  full(ln_g), full(ln_b), full(ws), full(bs), full(cw)],
        out_specs=(pl.BlockSpec((tm, w), lambda i: (i, 0)), pl.BlockSpec((tm, w), lambda i: (i, 0))),
        compiler_params=_params(("parallel",)),
        name="even_mix",
    )(hcat, hcat, hcat, hcat, hcat, hcat, hcat, hcat, hcat, ln_g, ln_b, ws, bs, cw)


def _out_proj_kernel(a1_ref, a2_ref, x_ref, w1_ref, w2_ref, g_ref, b_ref, o_ref):
    mix = _dot(a1_ref[...], w1_ref[...]) + _dot(a2_ref[...], w2_ref[...])
    o_ref[...] = _layer_norm(ALPHA * x_ref[...] + mix, g_ref[...], b_ref[...])


def _out_proj_ln(a1, a2, x, w, g, b, tm=512):
    m, d = x.shape
    k = a1.shape[1]
    assert a2.shape[1] == k and w.shape == (2 * k, d)
    full = lambda a: pl.BlockSpec(a.shape, lambda i: (0,) * a.ndim)
    return pl.pallas_call(
        _out_proj_kernel,
        out_shape=jax.ShapeDtypeStruct((m, d), F32),
        grid=(m // tm,),
        in_specs=[pl.BlockSpec((tm, k), lambda i: (i, 0)), pl.BlockSpec((tm, k), lambda i: (i, 0)),
                  pl.BlockSpec((tm, d), lambda i: (i, 0)),
                  pl.BlockSpec((k, d), lambda i: (0, 0)), pl.BlockSpec((k, d), lambda i: (1, 0)),
                  full(g), full(b)],
        out_specs=pl.BlockSpec((tm, d), lambda i: (i, 0)),
        compiler_params=_params(("parallel",)),
        name="out_proj_ln",
    )(a1, a2, x, w, w, g, b)


def _ffn_kernel(x_ref, w1_ref, w2_ref, g_ref, b_ref, o_ref, *rest):
    xb_ref = rest[-1]
    j = pl.program_id(1)

    @pl.when(j == 0)
    def _():
        x = x_ref[...]
        xb_ref[...] = x.astype(BF16)
        o_ref[...] = ALPHA * x

    h = jnp.maximum(_dot(xb_ref[...], w1_ref[...]), 0.0)
    o_ref[...] += _dot((h * h).astype(BF16), w2_ref[...])

    @pl.when(j == pl.num_programs(1) - 1)
    def _():
        y = _layer_norm(o_ref[...], g_ref[...], b_ref[...])
        o_ref[...] = y
        if len(rest) == 2:
            rest[0][...] = y.astype(BF16)


def _ffn_ln(x, w1, w2, g, b, emit_bf16, tm=512, tf=1024):
    m, d = x.shape
    f = w1.shape[1]
    full = lambda a: pl.BlockSpec(a.shape, lambda i, j: (0,) * a.ndim)
    out_shape = [jax.ShapeDtypeStruct((m, d), F32)]
    out_specs = [pl.BlockSpec((tm, d), lambda i, j: (i, 0))]
    if emit_bf16:
        out_shape.append(jax.ShapeDtypeStruct((m, d), BF16))
        out_specs.append(pl.BlockSpec((tm, d), lambda i, j: (i, 0)))
    return pl.pallas_call(
        _ffn_kernel,
        out_shape=out_shape,
        grid=(m // tm, f // tf),
        in_specs=[pl.BlockSpec((tm, d), lambda i, j: (i, 0)),
                  pl.BlockSpec((d, tf), lambda i, j: (0, j)),
                  pl.BlockSpec((tf, d), lambda i, j: (j, 0)),
                  full(g), full(b)],
        out_specs=out_specs,
        scratch_shapes=[pltpu.VMEM((tm, d), BF16)],
        compiler_params=_params(("parallel", "arbitrary")),
        name="ffn_ln",
    )(x, w1, w2, g, b)


def _rope(x, cos, sin_signed, is_first_half):
    partner = jnp.where(is_first_half, pltpu.roll(x, LANES - DH_D // 2, axis=1),
                        pltpu.roll(x, DH_D // 2, axis=1))
    return x * cos + partner * sin_signed


def _odd_in_kernel(x_ref, w_ref, cos_ref, sin_ref, xc_ref, qt_ref, k_ref, vt_ref, *, tq, tk):
    tm = x_ref.shape[0]
    j = pl.program_id(1)
    acc = _dot(x_ref[...], w_ref[...])
    lane = lax.broadcasted_iota(jnp.int32, (tm, LANES), 1)
    is_first_half = (lane % DH_D) < (DH_D // 2)

    @pl.when(j == 0)
    def _():
        xc_ref[...] = acc

    @pl.when(j == 1)
    def _():
        for h in range(H_D):
            cols = slice(h * LANES, (h + 1) * LANES)
            q = _rope(acc[:, cols], cos_ref[...], sin_ref[...], is_first_half) * Q_SCALE
            for c in range(tm // tq):
                qt_ref[c, cols, :] = q[c * tq:(c + 1) * tq, :].T.astype(BF16)

    @pl.when(j == 2)
    def _():
        for h in range(H_D):
            cols = slice(h * LANES, (h + 1) * LANES)
            k_ref[:, cols] = _rope(acc[:, cols], cos_ref[...], sin_ref[...], is_first_half).astype(BF16)

    @pl.when(j == 3)
    def _():
        pad_row = lax.broadcasted_iota(jnp.int32, (VT_ROWS - DV_D, tk), 0)
        pad = jnp.where(pad_row == 0, 1.0, 0.0).astype(BF16)
        for h in range(H_D):
            cols = slice(h * LANES, (h + 1) * LANES)
            for c in range(tm // tk):
                vt_ref[c, h * VT_ROWS:h * VT_ROWS + DV_D, :] = acc[c * tk:(c + 1) * tk, cols].T.astype(BF16)
                vt_ref[c, h * VT_ROWS + DV_D:(h + 1) * VT_ROWS, :] = pad


def _odd_in_proj(xb, w, cos, sin_signed, seq, tq, tk, tm=1024):
    m, d = xb.shape
    tn = D_C
    spt = seq // tm
    return pl.pallas_call(
        functools.partial(_odd_in_kernel, tq=tq, tk=tk),
        out_shape=(jax.ShapeDtypeStruct((m, D_C), F32),
                   jax.ShapeDtypeStruct((m // tq, D_D, tq), BF16),
                   jax.ShapeDtypeStruct((m, D_D), BF16),
                   jax.ShapeDtypeStruct((m // tk, H_D * VT_ROWS, tk), BF16)),
        grid=(m // tm, 4),
        in_specs=[pl.BlockSpec((tm, d), lambda i, j: (i, 0)),
                  pl.BlockSpec((d, tn), lambda i, j: (0, j)),
                  pl.BlockSpec((tm, LANES), lambda i, j: (i % spt, 0)),
                  pl.BlockSpec((tm, LANES), lambda i, j: (i % spt, 0))],
        out_specs=(pl.BlockSpec((tm, D_C), lambda i, j: (i, 0)),
                   pl.BlockSpec((tm // tq, D_D, tq), lambda i, j: (i, 0, 0)),
                   pl.BlockSpec((tm, D_D), lambda i, j: (i, 0)),
                   pl.BlockSpec((tm // tk, H_D * VT_ROWS, tk), lambda i, j: (i, 0, 0))),
        compiler_params=_params(("parallel", "arbitrary")),
        name="odd_in_proj",
    )(xb, w, cos, sin_signed)


def _fft_stage1_kernel(x_ref, f1h_ref, f1l_ref, wch_ref, wcl_ref, o_ref):
    n1 = x_ref.shape[1]
    groups = x_ref.shape[2] // LANES
    xh, xl = _split(x_ref[0])
    y = _dot3(f1h_ref[...], f1l_ref[...], xh, xl)
    yr = jnp.concatenate([y[:n1, g * LANES:(g + 1) * LANES] for g in range(groups)], axis=0)
    yi = jnp.concatenate([y[n1:, g * LANES:(g + 1) * LANES] for g in range(groups)], axis=0)
    ych, ycl = _split(jnp.concatenate([yr, yi], axis=1))
    u = _dot3(ych, ycl, wch_ref[...], wcl_ref[...])
    for g in range(groups):
        o_ref[0, 0, :, g * LANES:(g + 1) * LANES] = u[g * n1:(g + 1) * n1, :LANES]
        o_ref[0, 1, :, g * LANES:(g + 1) * LANES] = u[g * n1:(g + 1) * n1, LANES:]


def _fft_stage2_kernel(u_ref, f2c_ref, f2s_ref, tc_ref, ts_ref, o_ref, *, scale):
    kb = u_ref.shape[2]
    base = pl.program_id(1) * kb
    f2c = f2c_ref[...]
    f2s = f2s_ref[...]
    for kk in range(kb):
        tc = tc_ref[pl.ds(base + kk, 1), :]
        ts = ts_ref[pl.ds(base + kk, 1), :]
        gc = (f2c * tc - f2s * ts) * scale
        gs = (f2s * tc + f2c * ts) * scale
        gh, gl = _split(jnp.concatenate([gc, gs], axis=1))
        dh, dl = _split(jnp.concatenate([u_ref[0, 0, kk], u_ref[0, 1, kk]], axis=0))
        o_ref[0, :, kk * D_C:(kk + 1) * D_C] = _dot3(gh, gl, dh, dl).astype(BF16)


def _dft_tables(seq):
    n2 = CHUNK
    n1 = seq // n2
    two_pi = 2.0 * np.pi
    a1 = two_pi * np.outer(np.arange(n1), np.arange(n1)) / n1
    f1 = np.concatenate([np.cos(a1), -np.sin(a1)], axis=0)
    ac = two_pi * np.outer(np.arange(GC_W), np.arange(GC_W)) / GC_W
    wc = np.block([[np.cos(ac), -np.sin(ac)], [np.sin(ac), np.cos(ac)]])
    a2 = two_pi * np.outer(np.arange(n2), np.arange(n2)) / n2
    at = two_pi * np.outer(np.arange(n1), np.arange(n2)) / seq

    def split(a):
        a = jnp.asarray(a, F32)
        hi = a.astype(BF16)
        return hi, (a - hi.astype(F32)).astype(BF16)

    f32 = lambda a: jnp.asarray(a, F32)
    return split(f1) + split(wc) + (f32(np.cos(a2)), f32(np.sin(a2)), f32(np.cos(at)), f32(np.sin(at)))


def _fourier_mix(xc, batch, seq, jb=8, kb=4):
    n2 = CHUNK
    n1 = seq // n2
    f1h, f1l, wch, wcl, f2c, f2s, tc, ts = _dft_tables(seq)
    full2 = lambda a: pl.BlockSpec(a.shape, lambda b, i: (0,) * a.ndim)
    wblk = jb * D_C
    u = pl.pallas_call(
        _fft_stage1_kernel,
        out_shape=jax.ShapeDtypeStruct((batch, 2, n1, n2 * D_C), F32),
        grid=(batch, n2 // jb),
        in_specs=[pl.BlockSpec((1, n1, wblk), lambda b, i: (b, 0, i)),
                  full2(f1h), full2(f1l), full2(wch), full2(wcl)],
        out_specs=pl.BlockSpec((1, 2, n1, wblk), lambda b, i: (b, 0, 0, i)),
        compiler_params=_params(("parallel", "parallel")),
        name="fft_stage1",
    )(xc.reshape(batch, n1, n2 * D_C), f1h, f1l, wch, wcl)
    kb = min(kb, n1)
    yc = pl.pallas_call(
        functools.partial(_fft_stage2_kernel, scale=float(1.0 / math.sqrt(seq * GC_W))),
        out_shape=jax.ShapeDtypeStruct((batch, n2, n1 * D_C), BF16),
        grid=(batch, n1 // kb),
        in_specs=[pl.BlockSpec((1, 2, kb, n2, D_C), lambda b, i: (b, 0, i, 0, 0)),
                  full2(f2c), full2(f2s), full2(tc), full2(ts)],
        out_specs=pl.BlockSpec((1, n2, kb * D_C), lambda b, i: (b, 0, i)),
        compiler_params=_params(("parallel", "parallel")),
        name="fft_stage2",
    )(u.reshape(batch, 2, n1, n2, D_C), f2c, f2s, tc, ts)
    return yc.reshape(batch * seq, D_C)


def _attn_kernel(qt_ref, k_ref, vt_ref, lq1_ref, lk1_ref, lq2_ref, lk2_ref, g_ref, o_ref,
                 sa_ref, sb_ref, m_ref, acc_ref, *, lam_init):
    tq = qt_ref.shape[2]
    tk = vt_ref.shape[2]
    nk = vt_ref.shape[0]

    qt = qt_ref[0]
    row = lax.broadcasted_iota(jnp.int32, (LANES, tq), 0)
    zero = jnp.zeros_like(qt)
    q2 = jnp.concatenate([jnp.where(row < DH_D, qt, zero), jnp.where(row >= DH_D, qt, zero)], axis=1)

    m_ref[...] = jnp.full(m_ref.shape, -1e30, F32)
    acc_ref[...] = jnp.zeros(acc_ref.shape, F32)

    def scores(c):
        kc = k_ref[pl.ds(pl.multiple_of(c * tk, tk), tk), :]
        return _dot(kc, q2)

    def accumulate(s_ref, c):
        s = s_ref[...]
        m_old = m_ref[...]
        m_new = jnp.maximum(m_old, jnp.max(s, axis=0, keepdims=True))
        p = jnp.exp2(s - m_new).astype(BF16)
        acc_ref[...] = jnp.exp2(m_old - m_new) * acc_ref[...] + _dot(vt_ref[c], p)
        m_ref[...] = m_new

    bufs = (sa_ref, sb_ref)
    sa_ref[...] = scores(0)

    unroll = ATTN_UNROLL if nk >= 4 * ATTN_UNROLL else 2

    def run_chunks(c0, feed_next):
        for u in range(unroll):
            if u + 1 < unroll or feed_next:
                bufs[(u + 1) % 2][...] = scores(c0 + u + 1)
            accumulate(bufs[u % 2], c0 + u)

    def body(i, carry):
        run_chunks(unroll * i, True)
        return carry

    lax.fori_loop(0, nk // unroll - 1, body, 0)
    run_chunks(nk - unroll, False)

    lam = (jnp.exp(jnp.sum(lq1_ref[...] * lk1_ref[...], axis=1, keepdims=True))
           - jnp.exp(jnp.sum(lq2_ref[...] * lk2_ref[...], axis=1, keepdims=True)) + lam_init)
    on = acc_ref[:DV_D, :] / acc_ref[DV_D:DV_D + 1, :]
    d = on[:, :tq] - lam * on[:, tq:]
    ms = jnp.mean(d * d, axis=0, keepdims=True)
    y = d * lax.rsqrt(ms + LN_EPS) * g_ref[...] * (1.0 - lam_init)
    o_ref[...] = y.T.astype(BF16)


def _diff_attention(qt, k, vt, lq1, lk1, lq2, lk2, g, batch, seq, lam_init):
    tq = qt.shape[2]
    tk = vt.shape[2]
    m = batch * seq
    nq = seq // tq
    gb = jnp.broadcast_to(g.astype(F32).reshape(DV_D, 1), (DV_D, tq))
    vec = lambda a: a.astype(F32).reshape(1, DH_D)
    full = lambda a: pl.BlockSpec(a.shape, lambda b, h, i: (0,) * a.ndim)
    args = (qt, k, vt, vec(lq1), vec(lk1), vec(lq2), vec(lk2), gb)
    return pl.pallas_call(
        functools.partial(_attn_kernel, lam_init=lam_init),
        out_shape=jax.ShapeDtypeStruct((m, D_D), BF16),
        grid=(batch, H_D, nq),
        in_specs=[pl.BlockSpec((1, DV_D, tq), lambda b, h, i: (b * nq + i, h, 0)),
                  pl.BlockSpec((seq, DV_D), lambda b, h, i: (b, h)),
                  pl.BlockSpec((seq // tk, VT_ROWS, tk), lambda b, h, i: (b, h, 0)),
                  full(args[3]), full(args[4]), full(args[5]), full(args[6]), full(gb)],
        out_specs=pl.BlockSpec((tq, DV_D), lambda b, h, i: (b * nq + i, h)),
        scratch_shapes=[pltpu.VMEM((tk, 2 * tq), F32), pltpu.VMEM((tk, 2 * tq), F32),
                        pltpu.VMEM((1, 2 * tq), F32), pltpu.VMEM((VT_ROWS, 2 * tq), F32)],
        compiler_params=_params(("parallel", "parallel", "arbitrary")),
        name="diff_attention",
    )(*args)


def _rope_tables(seq):
    half = DH_D // 2
    inv_freq = 1.0 / (ROPE_THETA ** (jnp.arange(half, dtype=F32) / half))
    ang = jnp.arange(seq, dtype=F32)[:, None] * inv_freq[None, :]
    cos = jnp.tile(jnp.cos(ang), (1, LANES // half))
    sign = jnp.tile(jnp.concatenate([-jnp.ones((half,), F32), jnp.ones((half,), F32)]), LANES // DH_D)
    sin_signed = jnp.tile(jnp.sin(ang), (1, LANES // half)) * sign[None, :]
    return cos, sin_signed


def _trunk(x3, p):
    batch, seq, d = x3.shape
    x = x3.reshape(batch * seq, d)
    row = lambda a: a.astype(F32).reshape(1, -1)
    for l in range(DEPTH):
        i = l // 2
        if l % 2 == 0:
            hcat = _even_in_proj(x, p["w_in_ab"][i])
            bs = jnp.broadcast_to(p["sgu_b"][i].astype(F32)[:, :, None], (H_A, CHUNK, LANES))
            ya, yb = _even_mix(hcat, seq, row(p["sgu_ln_g"][i]), row(p["sgu_ln_b"][i]),
                               p["sgu_w"][i], bs, p["conv_w"][i].astype(F32))
            x = _out_proj_ln(ya, yb, x, p["w_out_ab"][i], row(p["ln1_g"][l]), row(p["ln1_b"][l]))
        else:
            lam_init = 0.8 - 0.6 * math.exp(-0.3 * l)
            cos, sin_signed = _rope_tables(seq)
            xc, qt, k, vt = _odd_in_proj(xb, p["w_in_cd"][i], cos, sin_signed, seq, tq=512, tk=512)
            yc = _fourier_mix(xc, batch, seq)
            yd = _diff_attention(qt, k, vt, p["lambda_q1"][i], p["lambda_k1"][i], p["lambda_q2"][i],
                                 p["lambda_k2"][i], p["subln_g"][i], batch, seq, lam_init)
            x = _out_proj_ln(yc, yd, x, p["w_out_cd"][i], row(p["ln1_g"][l]), row(p["ln1_b"][l]))
        last = l == DEPTH - 1
        outs = _ffn_ln(x, p["w_ff1"][l], p["w_ff2"][l], row(p["ln2_g"][l]), row(p["ln2_b"][l]),
                       emit_bf16=not last)
        x = outs[0]
        xb = None if last else outs[1]
    return x.reshape(batch, seq, d)


def kernel(x_prompt, x_sample, w_in_ab, sgu_ln_g, sgu_ln_b, sgu_w, sgu_b, conv_w, w_out_ab, w_in_cd,
           lambda_q1, lambda_k1, lambda_q2, lambda_k2, subln_g, w_out_cd, ln1_g, ln1_b, ln2_g, ln2_b,
           w_ff1, w_ff2):
    p = dict(w_in_ab=w_in_ab.astype(BF16), sgu_ln_g=sgu_ln_g, sgu_ln_b=sgu_ln_b, sgu_w=sgu_w.astype(BF16),
             sgu_b=sgu_b, conv_w=conv_w, w_out_ab=w_out_ab.astype(BF16), w_in_cd=w_in_cd.astype(BF16),
             lambda_q1=lambda_q1, lambda_k1=lambda_k1, lambda_q2=lambda_q2, lambda_k2=lambda_k2,
             subln_g=subln_g, w_out_cd=w_out_cd.astype(BF16), ln1_g=ln1_g, ln1_b=ln1_b, ln2_g=ln2_g,
             ln2_b=ln2_b, w_ff1=w_ff1.astype(BF16), w_ff2=w_ff2.astype(BF16))
    return (_trunk(x_prompt, p), _trunk(x_sample, p))
```
